```python
import math
import jax, jax.numpy as jnp
from jax import lax
import numpy as np

D_MODEL = 1024
BATCH = 8
SEQ = 4096
DEPTH = 2

HEAD_DIM = 64
A_HEADS = 6
A_PATTERNS = ((128, 1), (512, 4), (2048, 16))
B_HEADS = 5
B_QK_DIM = 32
B_V_DIM = 64
C_HEADS = 5
C_Q_RANK = 192
C_KV_RANK = 128
C_NOPE_DIM = 64
C_ROPE_DIM = 32
C_V_DIM = 64
C_ROPE_THETA = 10000.0
ROPE_THETA = 500000.0
ROPE_FRACTION = 4
IN_COLS = (A_HEADS * HEAD_DIM, A_HEADS * HEAD_DIM, A_HEADS * HEAD_DIM,
           B_HEADS * 2 * B_QK_DIM, B_HEADS * 2 * B_QK_DIM, B_HEADS * B_V_DIM,
           C_Q_RANK, C_KV_RANK, C_ROPE_DIM)
IN_WIDTH = sum(IN_COLS)
MIX_WIDTH = A_HEADS * HEAD_DIM + B_HEADS * B_V_DIM + C_HEADS * C_V_DIM
N_EXPERTS = 16
CAPACITY_FACTOR = 2
EXPERT_FF = 1408
Q_BLOCK = 128
NORM_EPS = 1e-6
SUBLN_EPS = 1e-5
NEG_INF = -1e30

kernel_name = 'hybrid_dilated_diff_mla_ec_encoder'


def rmsnorm(x, g, eps=NORM_EPS):
    xf = x.astype(jnp.float32)
    y = xf * lax.rsqrt(jnp.mean(xf * xf, axis=-1, keepdims=True) + eps)
    return (y * g.astype(jnp.float32)).astype(x.dtype)


def rope_angles(positions, rot_dim, theta):
    inv_freq = 1.0 / (theta ** (jnp.arange(0, rot_dim, 2, dtype=jnp.float32) / rot_dim))
    ang = positions.astype(jnp.float32)[:, :, None] * inv_freq
    return jnp.cos(ang)[:, :, None, :], jnp.sin(ang)[:, :, None, :]


def rotate(x, cos, sin):
    half = x.shape[-1] // 2
    x1 = x[..., :half].astype(jnp.float32)
    x2 = x[..., half:].astype(jnp.float32)
    return jnp.concatenate([x1 * cos - x2 * sin, x2 * cos + x1 * sin], axis=-1).astype(x.dtype)


def partial_rotate(x, cos, sin):
    r = 2 * cos.shape[-1]
    return jnp.concatenate([rotate(x[..., :r], cos, sin), x[..., r:]], axis=-1)


def dilated_window_stats(q, k, v, window, dilation):
    Bn, S, H, D = q.shape
    r = dilation
    n = window // (2 * dilation)
    L = S // r
    nb = -(-L // n)
    Lp = nb * n

    def residue_major(t):
        return t.reshape(Bn, L, r, H, D).transpose(0, 2, 3, 1, 4)

    qb = jnp.pad(residue_major(q), ((0, 0), (0, 0), (0, 0), (0, Lp - L), (0, 0))).reshape(Bn, r, H, nb, n, D)

    def key_windows(t):
        tp = jnp.pad(residue_major(t), ((0, 0), (0, 0), (0, 0), (n, Lp - L + n), (0, 0)))
        tp = tp.reshape(Bn, r, H, nb + 2, n, D)
        return jnp.concatenate([tp[:, :, :, :-2], tp[:, :, :, 1:-1], tp[:, :, :, 2:]], axis=4)

    kw = key_windows(k)
    vw = key_windows(v).astype(jnp.float32)
    s = jnp.einsum('brhiqd,brhikd->brhiqk', qb, kw, preferred_element_type=jnp.float32) * (D ** -0.5)
    qi = jnp.arange(n)[:, None]
    ki = jnp.arange(3 * n)[None, :]
    delta = ki - n - qi
    key_pos = jnp.arange(nb)[:, None, None] * n + ki[None] - n
    valid = (jnp.abs(delta) <= n)[None] & (key_pos >= 0) & (key_pos < L)
    s = jnp.where(valid, s, NEG_INF)
    m = jnp.max(s, axis=-1)
    p = jnp.exp(s - m[..., None])
    l = jnp.sum(p, axis=-1)
    o = jnp.einsum('brhiqk,brhikd->brhiqd', p, vw)

    def back(t):
        t = t.reshape((Bn, r, H, Lp) + t.shape[5:])[:, :, :, :L]
        t = jnp.moveaxis(t, 3, 1)
        return t.reshape((Bn, S, H) + t.shape[4:])

    return back(o), back(m), back(l)


def dilated_mixture_attention(q, k, v):
    stats = [dilated_window_stats(q, k, v, w, r) for (w, r) in A_PATTERNS]
    m_max = jnp.max(jnp.stack([st[1] for st in stats]), axis=0)
    num = 0.0
    den = 0.0
    for o, m, l in stats:
        e = jnp.exp(m - m_max)
        num = num + e[..., None] * o
        den = den + e * l
    return (num / den[..., None]).astype(q.dtype)


def to_blocks(t):
    Bn, S = t.shape[:2]
    return jnp.moveaxis(t.reshape((Bn, S // Q_BLOCK, Q_BLOCK) + t.shape[2:]), 1, 0)


def from_blocks(t):
    t = jnp.moveaxis(t, 0, 1)
    return t.reshape((t.shape[0], t.shape[1] * t.shape[2]) + t.shape[3:])


def differential_attention(q, k, v, lam):
    scale = q.shape[-1] ** -0.5
    vf = v.astype(jnp.float32)

    def one(qb):
        s = jnp.einsum('bqhcd,bkhcd->bhcqk', qb, k, preferred_element_type=jnp.float32) * scale
        p = jax.nn.softmax(s, axis=-1)
        w = p[:, :, 0] - lam * p[:, :, 1]
        return jnp.einsum('bhqk,bkhd->bqhd', w, vf)

    return from_blocks(lax.map(one, to_blocks(q))).astype(v.dtype)


def latent_attention(q_nope, q_rope, k_nope, k_rope, v):
    scale = (q_nope.shape[-1] + q_rope.shape[-1]) ** -0.5
    vf = v.astype(jnp.float32)

    def one(blk):
        qn, qr = blk
        s = (jnp.einsum('bqhd,bkhd->bhqk', qn, k_nope, preferred_element_type=jnp.float32)
             + jnp.einsum('bqhr,bkr->bhqk', qr, k_rope, preferred_element_type=jnp.float32)) * scale
        p = jax.nn.softmax(s, axis=-1)
        return jnp.einsum('bhqk,bkhd->bqhd', p, vf)

    return from_blocks(lax.map(one, (to_blocks(q_nope), to_blocks(q_rope)))).astype(v.dtype)


def hybrid_mixer(h, positions, w_in, lam_q1, lam_k1, lam_q2, lam_k2, diff_subln_g, lam_init,
                 q_norm_g, w_uq, kv_norm_g, w_ukv, a_out_g, c_out_g, w_out):
    Bn, S, _ = h.shape
    proj = jnp.einsum('bsd,de->bse', h, w_in)
    split_points = np.cumsum(IN_COLS)[:-1].tolist()
    aq, ak, av, bq, bk, bv, cq, ckv, ckr = jnp.split(proj, split_points, axis=-1)

    cos_a, sin_a = rope_angles(positions, HEAD_DIM // ROPE_FRACTION, ROPE_THETA)
    aq = partial_rotate(aq.reshape(Bn, S, A_HEADS, HEAD_DIM), cos_a, sin_a)
    ak = partial_rotate(ak.reshape(Bn, S, A_HEADS, HEAD_DIM), cos_a, sin_a)
    av = av.reshape(Bn, S, A_HEADS, HEAD_DIM)
    oa = dilated_mixture_attention(aq, ak, av)
    oa = rmsnorm(oa, a_out_g.reshape(A_HEADS, HEAD_DIM))

    cos_b, sin_b = rope_angles(positions, B_QK_DIM // ROPE_FRACTION, ROPE_THETA)
    bq = partial_rotate(bq.reshape(Bn, S, 2 * B_HEADS, B_QK_DIM), cos_b, sin_b).reshape(Bn, S, B_HEADS, 2, B_QK_DIM)
    bk = partial_rotate(bk.reshape(Bn, S, 2 * B_HEADS, B_QK_DIM), cos_b, sin_b).reshape(Bn, S, B_HEADS, 2, B_QK_DIM)
    bv = bv.reshape(Bn, S, B_HEADS, B_V_DIM)
    f32 = jnp.float32
    lam = (jnp.exp(jnp.sum(lam_q1.astype(f32) * lam_k1.astype(f32)))
           - jnp.exp(jnp.sum(lam_q2.astype(f32) * lam_k2.astype(f32))) + lam_init)
    ob = differential_attention(bq, bk, bv, lam)
    ob = rmsnorm(ob, diff_subln_g * (1.0 - lam_init), eps=SUBLN_EPS)

    cos_c, sin_c = rope_angles(positions, C_ROPE_DIM, C_ROPE_THETA)
    qc = jnp.einsum('bsr,re->bse', rmsnorm(cq, q_norm_g), w_uq).reshape(Bn, S, C_HEADS, C_NOPE_DIM + C_ROPE_DIM)
    q_nope = qc[..., :C_NOPE_DIM]
    q_rope = rotate(qc[..., C_NOPE_DIM:], cos_c, sin_c)
    kv = jnp.einsum('bsr,re->bse', rmsnorm(ckv, kv_norm_g), w_ukv).reshape(Bn, S, C_HEADS, C_NOPE_DIM + C_V_DIM)
    k_nope = kv[..., :C_NOPE_DIM]
    vc = kv[..., C_NOPE_DIM:]
    k_rope = rotate(ckr[:, :, None, :], cos_c, sin_c)[:, :, 0]
    oc = latent_attention(q_nope, q_rope, k_nope, k_rope, vc)
    oc = rmsnorm(oc, c_out_g.reshape(C_HEADS, C_V_DIM))

    o = jnp.concatenate([oa.reshape(Bn, S, -1), ob.reshape(Bn, S, -1), oc.reshape(Bn, S, -1)], axis=-1)
    return jnp.einsum('bse,ed->bsd', o, w_out)


def expert_choice_ffn(h, w_router, w_gate, w_up, w_down):
    Bn, S, D = h.shape
    cap = CAPACITY_FACTOR * S // N_EXPERTS
    logits = jnp.einsum('bsd,de->bse', h, w_router, preferred_element_type=jnp.float32)
    affinity = jax.nn.softmax(logits, axis=-1)
    gate, idx = lax.top_k(jnp.swapaxes(affinity, 1, 2), cap)
    xe = jax.vmap(lambda hb, ib: hb[ib])(h, idx)
    a = jnp.einsum('becd,edf->becf', xe, w_gate)
    u = jnp.einsum('becd,edf->becf', xe, w_up)
    y = jnp.einsum('becf,efd->becd', jax.nn.silu(a) * u, w_down)
    y = y * gate[..., None].astype(y.dtype)

    def scatter(yb, ib):
        return jnp.zeros((S, D), yb.dtype).at[ib.reshape(-1)].add(yb.reshape(-1, D))

    return jax.vmap(scatter)(y, idx)


def setup_inputs(seed: int = 0) -> dict:
    key = jax.random.key(seed)
    ks = jax.random.split(key, 24)
    f32 = jnp.float32

    def nrm(k, shape, fan_in):
        return jax.random.normal(k, shape, f32) * (fan_in ** -0.5)

    def gain(k, shape):
        return 1.0 + 0.02 * jax.random.normal(k, shape, f32)

    x = jax.random.normal(ks[0], (BATCH, SEQ, D_MODEL), f32)
    offsets = jax.random.randint(ks[1], (BATCH, 1), 0, 1024)
    positions = (jnp.arange(SEQ, dtype=jnp.int32)[None, :] + offsets).astype(jnp.int32)
    return {
        'x': x,
        'positions': positions,
        'attn_norm_g': gain(ks[2], (DEPTH, D_MODEL)),
        'w_in': nrm(ks[3], (DEPTH, D_MODEL, IN_WIDTH), D_MODEL),
        'lam_q1': 0.1 * jax.random.normal(ks[4], (DEPTH, B_QK_DIM), f32),
        'lam_k1': 0.1 * jax.random.normal(ks[5], (DEPTH, B_QK_DIM), f32),
        'lam_q2': 0.1 * jax.random.normal(ks[6], (DEPTH, B_QK_DIM), f32),
        'lam_k2': 0.1 * jax.random.normal(ks[7], (DEPTH, B_QK_DIM), f32),
        'diff_subln_g': gain(ks[8], (DEPTH, B_V_DIM)),
        'mla_q_norm_g': gain(ks[9], (DEPTH, C_Q_RANK)),
        'mla_w_uq': nrm(ks[10], (DEPTH, C_Q_RANK, C_HEADS * (C_NOPE_DIM + C_ROPE_DIM)), C_Q_RANK),
        'mla_kv_norm_g': gain(ks[11], (DEPTH, C_KV_RANK)),
        'mla_w_ukv': nrm(ks[12], (DEPTH, C_KV_RANK, C_HEADS * (C_NOPE_DIM + C_V_DIM)), C_KV_RANK),
        'dil_out_g': gain(ks[13], (DEPTH, A_HEADS * HEAD_DIM)),
        'mla_out_g': gain(ks[14], (DEPTH, C_HEADS * C_V_DIM)),
        'w_out': nrm(ks[15], (DEPTH, MIX_WIDTH, D_MODEL), MIX_WIDTH),
        'ffn_norm_g': gain(ks[16], (DEPTH, D_MODEL)),
        'w_router': nrm(ks[17], (DEPTH, D_MODEL, N_EXPERTS), D_MODEL),
        'w_gate': nrm(ks[18], (DEPTH, N_EXPERTS, D_MODEL, EXPERT_FF), D_MODEL),
        'w_up': nrm(ks[19], (DEPTH, N_EXPERTS, D_MODEL, EXPERT_FF), D_MODEL),
        'w_down': nrm(ks[20], (DEPTH, N_EXPERTS, EXPERT_FF, D_MODEL), EXPERT_FF),
        'final_norm_g': gain(ks[21], (D_MODEL,)),
    }


def reference(x, positions, attn_norm_g, w_in, lam_q1, lam_k1, lam_q2, lam_k2, diff_subln_g,
              mla_q_norm_g, mla_w_uq, mla_kv_norm_g, mla_w_ukv, dil_out_g, mla_out_g, w_out,
              ffn_norm_g, w_router, w_gate, w_up, w_down, final_norm_g):
    for l in range(DEPTH):
        lam_init = 0.8 - 0.6 * math.exp(-0.3 * l)
        h = rmsnorm(x, attn_norm_g[l])
        x = x + hybrid_mixer(h, positions, w_in[l], lam_q1[l], lam_k1[l], lam_q2[l], lam_k2[l],
                             diff_subln_g[l], lam_init, mla_q_norm_g[l], mla_w_uq[l],
                             mla_kv_norm_g[l], mla_w_ukv[l], dil_out_g[l], mla_out_g[l], w_out[l])
        h = rmsnorm(x, ffn_norm_g[l])
        x = x + expert_choice_ffn(h, w_router[l], w_gate[l], w_up[l], w_down[l])
    return rmsnorm(x, final_norm_g)
```

```python
import functools
import math

import jax
import jax.numpy as jnp
from jax import lax
from jax.experimental import pallas as pl
from jax.experimental.pallas import tpu as pltpu

F32 = jnp.float32
BF16 = jnp.bfloat16
LANES = 128

D_MODEL = 1024
HEAD_DIM = 64
A_HEADS = 6
A_DILATIONS = (1, 4, 16)
A_HALF_WINDOW = 64
B_HEADS = 5
B_QK_DIM = 32
B_V_DIM = 64
C_HEADS = 5
C_Q_RANK = 192
C_KV_RANK = 128
C_NOPE_DIM = 64
C_ROPE_DIM = 32
C_V_DIM = 64
C_ROPE_THETA = 10000.0
ROPE_THETA = 500000.0
ROPE_FRACTION = 4
N_EXPERTS = 16
CAPACITY_FACTOR = 2
EXPERT_FF = 1408
NORM_EPS = 1e-6
SUBLN_EPS = 1e-5
NEG_INF = -1e30

A_W = A_HEADS * HEAD_DIM
B_SLABS = 3
B_W = B_SLABS * LANES
C_W = C_HEADS * LANES
CQ_PAD = 256
PROJ_W = 3 * A_W + 3 * B_W + CQ_PAD + LANES + LANES
N_TABS = 9

VMEM_LIMIT = 56 * 1024 * 1024

_NT = (((1,), (1,)), ((), ()))


def _cparams(sem):
    return pltpu.CompilerParams(dimension_semantics=sem, vmem_limit_bytes=VMEM_LIMIT)


def _rope_tables(positions):
    pos = positions.astype(F32)

    def angles(rot_dim, theta):
        inv = 1.0 / (theta ** (jnp.arange(0, rot_dim, 2, dtype=F32) / rot_dim))
        ang = pos[:, :, None] * inv
        return jnp.cos(ang), jnp.sin(ang)

    def pattern(cos, sin, period, offset):
        half = cos.shape[-1]
        bsz, seq = cos.shape[:2]
        one = jnp.ones((bsz, seq, period), F32)
        zero = jnp.zeros((bsz, seq, period), F32)
        c = one.at[:, :, offset:offset + half].set(cos).at[:, :, offset + half:offset + 2 * half].set(cos)
        s1 = zero.at[:, :, offset:offset + half].set(-sin)
        s2 = zero.at[:, :, offset + half:offset + 2 * half].set(sin)
        reps = LANES // period
        return [jnp.tile(t, (1, 1, reps)) for t in (c, s1, s2)]

    cos_a, sin_a = angles(HEAD_DIM // ROPE_FRACTION, ROPE_THETA)
    cos_b, sin_b = angles(B_QK_DIM // ROPE_FRACTION, ROPE_THETA)
    cos_c, sin_c = angles(C_ROPE_DIM, C_ROPE_THETA)
    tabs = (pattern(cos_a, sin_a, HEAD_DIM, 0) + pattern(cos_b, sin_b, B_QK_DIM, 0)
            + pattern(cos_c, sin_c, LANES, C_NOPE_DIM))
    return jnp.concatenate(tabs, axis=-1)


def _pad_cols(w, width):
    return jnp.pad(w, ((0, 0), (0, width - w.shape[1])))


def _prep_layer(w_in, mla_q_norm_g, mla_w_uq, mla_kv_norm_g, mla_w_ukv, w_out):
    a = A_W
    bqk = B_HEADS * 2 * B_QK_DIM
    bv = B_HEADS * B_V_DIM
    o = 0
    aq = w_in[:, o:o + a] * (HEAD_DIM ** -0.5); o += a
    ak = w_in[:, o:o + a]; o += a
    av = w_in[:, o:o + a]; o += a
    bq = w_in[:, o:o + bqk] * (B_QK_DIM ** -0.5); o += bqk
    bk = w_in[:, o:o + bqk]; o += bqk
    bvv = w_in[:, o:o + bv]; o += bv
    cq = w_in[:, o:o + C_Q_RANK]; o += C_Q_RANK
    ckv = w_in[:, o:o + C_KV_RANK]; o += C_KV_RANK
    ckr = w_in[:, o:o + C_ROPE_DIM]
    ckr_slab = jnp.pad(ckr, ((0, 0), (C_NOPE_DIM, LANES - C_NOPE_DIM - C_ROPE_DIM)))
    w_proj = jnp.concatenate([aq, ak, av, _pad_cols(bq, B_W), _pad_cols(bk, B_W), _pad_cols(bvv, B_W),
                              _pad_cols(cq, CQ_PAD), ckv, ckr_slab], axis=1).astype(BF16)

    qd = C_NOPE_DIM + C_ROPE_DIM
    wuq = mla_w_uq.reshape(C_Q_RANK, C_HEADS, qd) * (qd ** -0.5)
    wuq = jnp.pad(wuq, ((0, CQ_PAD - C_Q_RANK), (0, 0), (0, LANES - qd))).reshape(CQ_PAD, C_W).astype(BF16)
    wukv = mla_w_ukv.reshape(C_KV_RANK, C_HEADS, C_NOPE_DIM + C_V_DIM)
    wk = jnp.pad(wukv[:, :, :C_NOPE_DIM], ((0, 0), (0, 0), (0, LANES - C_NOPE_DIM))).reshape(C_KV_RANK, C_W)
    wv = jnp.pad(wukv[:, :, C_NOPE_DIM:], ((0, 0), (0, 0), (0, LANES - C_V_DIM))).reshape(C_KV_RANK, C_W)
    wukv_p = jnp.concatenate([wk, wv], axis=1).astype(BF16)
    gq = jnp.pad(mla_q_norm_g, (0, CQ_PAD - C_Q_RANK)).reshape(1, CQ_PAD)
    gkv = mla_kv_norm_g.reshape(1, C_KV_RANK)

    wo_a = w_out[:a].astype(BF16)
    wo_b = jnp.pad(w_out[a:a + bv], ((0, B_W - bv), (0, 0))).astype(BF16)
    wo_c = w_out[a + bv:].reshape(C_HEADS, C_V_DIM, D_MODEL)
    wo_c = jnp.pad(wo_c, ((0, 0), (0, LANES - C_V_DIM), (0, 0))).reshape(C_W, D_MODEL).astype(BF16)
    return w_proj, gq, wuq, gkv, wukv_p, wo_a, wo_b, wo_c


def _proj_kernel(x_ref, g_ref, w_ref, tab_ref, gq_ref, wuq_ref, gkv_ref, wukv_ref,
                 qa_ref, qb_ref, qc_ref, kc_ref, vc_ref):
    x = x_ref[0]
    ms = jnp.mean(x * x, axis=-1, keepdims=True)
    h = (x * lax.rsqrt(ms + NORM_EPS) * g_ref[...]).astype(BF16)
    p = jnp.dot(h, w_ref[...], preferred_element_type=F32)

    def slab(i):
        return p[:, i * LANES:(i + 1) * LANES]

    def rope(xs, layout, shift):
        c = tab_ref[0, :, (3 * layout) * LANES:(3 * layout + 1) * LANES]
        s1 = tab_ref[0, :, (3 * layout + 1) * LANES:(3 * layout + 2) * LANES]
        s2 = tab_ref[0, :, (3 * layout + 2) * LANES:(3 * layout + 3) * LANES]
        return xs * c + pltpu.roll(xs, LANES - shift, 1) * s1 + pltpu.roll(xs, shift, 1) * s2

    n_a = A_W // LANES
    for i in range(3 * n_a):
        v = slab(i)
        if i < 2 * n_a:
            v = rope(v, 0, HEAD_DIM // ROPE_FRACTION // 2)
        qa_ref[0, :, i * LANES:(i + 1) * LANES] = v
    base = 3 * n_a
    for i in range(3 * B_SLABS):
        v = slab(base + i)
        if i < 2 * B_SLABS:
            v = rope(v, 1, B_QK_DIM // ROPE_FRACTION // 2)
        qb_ref[0, :, i * LANES:(i + 1) * LANES] = v.astype(BF16)
    base += 3 * B_SLABS

    cq = p[:, base * LANES:base * LANES + CQ_PAD]
    msq = jnp.sum(cq * cq, axis=-1, keepdims=True) * (1.0 / C_Q_RANK)
    cqn = (cq * lax.rsqrt(msq + NORM_EPS) * gq_ref[...]).astype(BF16)
    qf = jnp.dot(cqn, wuq_ref[...], preferred_element_type=F32)
    base += CQ_PAD // LANES
    ckv = slab(base)
    mskv = jnp.mean(ckv * ckv, axis=-1, keepdims=True)
    ckvn = (ckv * lax.rsqrt(mskv + NORM_EPS) * gkv_ref[...]).astype(BF16)
    kvf = jnp.dot(ckvn, wukv_ref[...], preferred_element_type=F32)
    kr = rope(slab(base + 1), 2, C_ROPE_DIM // 2)
    for hd in range(C_HEADS):
        sl = slice(hd * LANES, (hd + 1) * LANES)
        qc_ref[0, :, sl] = rope(qf[:, sl], 2, C_ROPE_DIM // 2).astype(BF16)
        kc_ref[0, :, sl] = (kvf[:, sl] + kr).astype(BF16)
        vc_ref[0, :, sl] = kvf[:, C_W + hd * LANES:C_W + (hd + 1) * LANES].astype(BF16)


def _proj_call(x, g, w_proj, tabs, gq, wuq, gkv, wukv, tm=256):
    bsz, seq, d = x.shape
    grid = (bsz, seq // tm)
    row = lambda b, i: (b, i, 0)
    const = lambda b, i: (0, 0)
    out_w = (3 * A_W, 3 * B_W, C_W, C_W, C_W)
    out_dt = (F32, BF16, BF16, BF16, BF16)
    return pl.pallas_call(
        _proj_kernel,
        grid=grid,
        in_specs=[pl.BlockSpec((1, tm, d), row),
                  pl.BlockSpec((1, d), const),
                  pl.BlockSpec((d, PROJ_W), const),
                  pl.BlockSpec((1, tm, N_TABS * LANES), row),
                  pl.BlockSpec((1, CQ_PAD), const),
                  pl.BlockSpec((CQ_PAD, C_W), const),
                  pl.BlockSpec((1, C_KV_RANK), const),
                  pl.BlockSpec((C_KV_RANK, 2 * C_W), const)],
        out_specs=[pl.BlockSpec((1, tm, w), row) for w in out_w],
        out_shape=[jax.ShapeDtypeStruct((bsz, seq, w), dt) for w, dt in zip(out_w, out_dt)],
        compiler_params=_cparams(("parallel", "parallel")),
        name="proj",
    )(x, g.reshape(1, d), w_proj, tabs, gq, wuq, gkv, wukv)


def _head_rmsnorm(o, g, eps):
    ri = lax.broadcasted_iota(jnp.int32, (LANES, LANES), 0) // HEAD_DIM
    ci = lax.broadcasted_iota(jnp.int32, (LANES, LANES), 1) // HEAD_DIM
    bd = (ri == ci).astype(F32)
    ss = jnp.dot(o * o, bd, preferred_element_type=F32, precision=lax.Precision.HIGHEST)
    return o * lax.rsqrt(ss * (1.0 / HEAD_DIM) + eps) * g


def _softmax_pv(qm, k_ref, v_ref, s_scr, tk):
    tq = qm.shape[0]
    nchunks = k_ref.shape[1] // tk
    mpart = jnp.full((tq, LANES), -jnp.inf, F32)
    for c in range(nchunks):
        s = lax.dot_general(qm, k_ref[0, c * tk:(c + 1) * tk, :], _NT, preferred_element_type=F32)
        s_scr[c] = s
        for j in range(tk // LANES):
            mpart = jnp.maximum(mpart, s[:, j * LANES:(j + 1) * LANES])
    m = jnp.max(mpart, axis=1, keepdims=True)
    lpart = jnp.zeros((tq, LANES), F32)
    acc = jnp.zeros((tq, LANES), F32)
    for c in range(nchunks):
        p = jnp.exp(s_scr[c] - m)
        for j in range(tk // LANES):
            lpart = lpart + p[:, j * LANES:(j + 1) * LANES]
        acc = acc + jnp.dot(p.astype(BF16), v_ref[0, c * tk:(c + 1) * tk, :], preferred_element_type=F32)
    l = jnp.sum(lpart, axis=1, keepdims=True)
    return acc / l


_DIL_TQ = 128
_DIL_TW = 256


def _dil_kernel(q_ref, k_ref, v_ref, g_ref, o_ref, *scr):
    seq = q_ref.shape[1]
    lane = lax.broadcasted_iota(jnp.int32, (1, LANES), 1)
    head0 = lane < HEAD_DIM
    rel = (lax.broadcasted_iota(jnp.int32, (_DIL_TQ, _DIL_TW), 1)
           - lax.broadcasted_iota(jnp.int32, (_DIL_TQ, _DIL_TW), 0))

    for pi, r in enumerate(A_DILATIONS):
        length = seq // r
        nblk = length // _DIL_TQ
        o_scr, m_scr, l_scr = scr[3 * pi:3 * pi + 3]

        def body(t, carry, r=r, length=length, nblk=nblk, o_scr=o_scr, m_scr=m_scr, l_scr=l_scr):
            c = t // nblk
            i = t % nblk
            ws = jnp.clip(i * _DIL_TQ - A_HALF_WINDOW, 0, length - _DIL_TW)
            qs = c + r * (i * _DIL_TQ)
            ks = c + r * ws
            if r == 1:
                qsl = pl.ds(pl.multiple_of(qs, _DIL_TQ), _DIL_TQ)
                ksl = pl.ds(pl.multiple_of(ks, A_HALF_WINDOW), _DIL_TW)
            else:
                qsl = pl.ds(qs, _DIL_TQ, stride=r)
                ksl = pl.ds(ks, _DIL_TW, stride=r)
            qb = q_ref[0, qsl, :]
            kb = k_ref[0, ksl, :].astype(BF16)
            vb = v_ref[0, ksl, :].astype(BF16)
            delta = rel + (ws - i * _DIL_TQ)
            valid = jnp.abs(delta) <= A_HALF_WINDOW
            res = []
            for hd in range(2):
                qm = jnp.where(head0 if hd == 0 else jnp.logical_not(head0), qb, 0.0).astype(BF16)
                s = lax.dot_general(qm, kb, _NT, preferred_element_type=F32)
                s = jnp.where(valid, s, NEG_INF)
                m = jnp.max(s, axis=1, keepdims=True)
                p = jnp.exp(s - m)
                l = jnp.sum(p, axis=1, keepdims=True)
                o = jnp.dot(p.astype(BF16), vb, preferred_element_type=F32)
                res.append((o, m, l))
            o_scr[qsl, :] = jnp.where(head0, res[0][0], res[1][0])
            m_scr[qsl, :] = jnp.where(head0, res[0][1], res[1][1])
            l_scr[qsl, :] = jnp.where(head0, res[0][2], res[1][2])
            return carry

        lax.fori_loop(0, r * nblk, body, 0)

    rows = 256
    g = g_ref[...]

    def combine(t, carry):
        sl = pl.ds(pl.multiple_of(t * rows, rows), rows)
        ms = [scr[3 * pi + 1][sl, :] for pi in range(3)]
        mm = jnp.maximum(jnp.maximum(ms[0], ms[1]), ms[2])
        num = jnp.zeros((rows, LANES), F32)
        den = jnp.zeros((rows, LANES), F32)
        for pi in range(3):
            e = jnp.exp(ms[pi] - mm)
            num = num + e * scr[3 * pi][sl, :]
            den = den + e * scr[3 * pi + 2][sl, :]
        o_ref[0, sl, :] = _head_rmsnorm(num / den, g, NORM_EPS).astype(o_ref.dtype)
        return carry

    lax.fori_loop(0, seq // rows, combine, 0)


def _dil_call(qkv_a, g):
    bsz, seq, _ = qkv_a.shape
    n_slab = A_W // LANES
    blk = (1, seq, LANES)
    return pl.pallas_call(
        _dil_kernel,
        grid=(bsz, n_slab),
        in_specs=[pl.BlockSpec(blk, lambda b, j: (b, 0, j)),
                  pl.BlockSpec(blk, lambda b, j: (b, 0, n_slab + j)),
                  pl.BlockSpec(blk, lambda b, j: (b, 0, 2 * n_slab + j)),
                  pl.BlockSpec((1, LANES), lambda b, j: (0, j))],
        out_specs=pl.BlockSpec(blk, lambda b, j: (b, 0, j)),
        out_shape=jax.ShapeDtypeStruct((bsz, seq, A_W), BF16),
        scratch_shapes=[pltpu.VMEM((seq, LANES), F32) for _ in range(9)],
        compiler_params=_cparams(("parallel", "parallel")),
        name="dilated_attn",
    )(qkv_a, qkv_a, qkv_a, g.reshape(1, A_W))


_ATT_TQ = 256
_ATT_TK = 512


def _diff_kernel(q_ref, k_ref, v_ref, lam_ref, g_ref, o_ref, s_scr):
    lane = lax.broadcasted_iota(jnp.int32, (1, LANES), 1)
    q = q_ref[0]
    lam = lam_ref[...]

    def head(hd):
        outs = []
        for mp in range(2):
            lo = hd * HEAD_DIM + mp * B_QK_DIM
            qm = jnp.where((lane >= lo) & (lane < lo + B_QK_DIM), q, jnp.zeros_like(q))
            outs.append(_softmax_pv(qm, k_ref, v_ref, s_scr.at[mp], _ATT_TK))
        return outs[0] - lam * outs[1]

    o_ref[0] = jnp.where(lane < HEAD_DIM, head(0), 0.0)

    @pl.when(pl.program_id(1) < B_HEADS // 2)
    def _():
        o_ref[0] = jnp.where(lane < HEAD_DIM, o_ref[0], head(1))

    o_ref[0] = _head_rmsnorm(o_ref[0], g_ref[...], SUBLN_EPS)


def _diff_call(qkv_b, lam, g):
    bsz, seq, _ = qkv_b.shape
    nq = seq // _ATT_TQ
    kv = (1, seq, LANES)
    out = pl.pallas_call(
        _diff_kernel,
        grid=(bsz, B_SLABS, nq),
        in_specs=[pl.BlockSpec((1, _ATT_TQ, LANES), lambda b, j, i: (b, i, j)),
                  pl.BlockSpec(kv, lambda b, j, i: (b, 0, B_SLABS + j)),
                  pl.BlockSpec(kv, lambda b, j, i: (b, 0, 2 * B_SLABS + j)),
                  pl.BlockSpec((1, LANES), lambda b, j, i: (0, 0)),
                  pl.BlockSpec((1, LANES), lambda b, j, i: (0, 0))],
        out_specs=pl.BlockSpec((1, _ATT_TQ, LANES), lambda b, j, i: (b, i, j)),
        out_shape=jax.ShapeDtypeStruct((bsz, seq, B_W), F32),
        scratch_shapes=[pltpu.VMEM((2, seq // _ATT_TK, _ATT_TQ, _ATT_TK), F32)],
        compiler_params=_cparams(("parallel", "parallel", "arbitrary")),
        name="diff_attn",
    )(qkv_b, qkv_b, qkv_b, lam, g)
    return out


def _mla_kernel(q_ref, k_ref, v_ref, g_ref, o_ref, s_scr):
    o = _softmax_pv(q_ref[0], k_ref, v_ref, s_scr, _ATT_TK)
    o_ref[0] = _head_rmsnorm(o, g_ref[...], NORM_EPS).astype(o_ref.dtype)


def _mla_call(qc, kc, vc, g):
    bsz, seq, _ = qc.shape
    nq = seq // _ATT_TQ
    kv = (1, seq, LANES)
    return pl.pallas_call(
        _mla_kernel,
        grid=(bsz, C_HEADS, nq),
        in_specs=[pl.BlockSpec((1, _ATT_TQ, LANES), lambda b, j, i: (b, i, j)),
                  pl.BlockSpec(kv, lambda b, j, i: (b, 0, j)),
                  pl.BlockSpec(kv, lambda b, j, i: (b, 0, j)),
                  pl.BlockSpec((1, LANES), lambda b, j, i: (0, j))],
        out_specs=pl.BlockSpec((1, _ATT_TQ, LANES), lambda b, j, i: (b, i, j)),
        out_shape=jax.ShapeDtypeStruct((bsz, seq, C_W), BF16),
        scratch_shapes=[pltpu.VMEM((seq // _ATT_TK, _ATT_TQ, _ATT_TK), F32)],
        compiler_params=_cparams(("parallel", "parallel", "arbitrary")),
        name="latent_attn",
    )(qc, kc, vc, g)


def _out_kernel(oa_ref, ob_ref, oc_ref, x_ref, wa_ref, wb_ref, wc_ref, g_ref, wr_ref,
                x1_ref, h_ref, lg_ref):
    y = jnp.dot(oa_ref[0], wa_ref[...], preferred_element_type=F32)
    y = y + jnp.dot(ob_ref[0].astype(BF16), wb_ref[...], preferred_element_type=F32)
    y = y + jnp.dot(oc_ref[0], wc_ref[...], preferred_element_type=F32)
    x1 = x_ref[0] + y
    x1_ref[0] = x1
    ms = jnp.mean(x1 * x1, axis=-1, keepdims=True)
    h = x1 * lax.rsqrt(ms + NORM_EPS) * g_ref[...]
    h_ref[0] = h
    lg_ref[0] = jnp.dot(h, wr_ref[...], preferred_element_type=F32, precision=lax.Precision.HIGHEST)


def _out_call(oa, ob, oc, x, wo_a, wo_b, wo_c, g, w_router, tm=256):
    bsz, seq, d = x.shape
    row = lambda b, i: (b, i, 0)
    const = lambda b, i: (0, 0)
    wr = _pad_cols(w_router, LANES)
    return pl.pallas_call(
        _out_kernel,
        grid=(bsz, seq // tm),
        in_specs=[pl.BlockSpec((1, tm, A_W), row),
                  pl.BlockSpec((1, tm, B_W), row),
                  pl.BlockSpec((1, tm, C_W), row),
                  pl.BlockSpec((1, tm, d), row),
                  pl.BlockSpec((A_W, d), const),
                  pl.BlockSpec((B_W, d), const),
                  pl.BlockSpec((C_W, d), const),
                  pl.BlockSpec((1, d), const),
                  pl.BlockSpec((d, LANES), const)],
        out_specs=[pl.BlockSpec((1, tm, d), row),
                   pl.BlockSpec((1, tm, d), row),
                   pl.BlockSpec((1, tm, LANES), row)],
        out_shape=[jax.ShapeDtypeStruct((bsz, seq, d), F32),
                   jax.ShapeDtypeStruct((bsz, seq, d), F32),
                   jax.ShapeDtypeStruct((bsz, seq, LANES), F32)],
        compiler_params=_cparams(("parallel", "parallel")),
        name="out_proj",
    )(oa, ob, oc, x, wo_a, wo_b, wo_c, g.reshape(1, d), wr)


def _expert_kernel(xe_ref, gate_ref, wgu_ref, wd_ref, y_ref):
    au = jnp.dot(xe_ref[0], wgu_ref[0], preferred_element_type=F32)
    a = au[:, :EXPERT_FF]
    u = au[:, EXPERT_FF:]
    hmid = (a * jax.nn.sigmoid(a) * u).astype(BF16)
    y = jnp.dot(hmid, wd_ref[0], preferred_element_type=F32)
    y_ref[0] = y * gate_ref[0]


def _expert_call(xe, gate, wgu, wd, tm=512):
    n_e, m, d = xe.shape
    return pl.pallas_call(
        _expert_kernel,
        grid=(n_e, m // tm),
        in_specs=[pl.BlockSpec((1, tm, d), lambda e, i: (e, i, 0)),
                  pl.BlockSpec((1, tm, 1), lambda e, i: (e, i, 0)),
                  pl.BlockSpec((1, d, 2 * EXPERT_FF), lambda e, i: (e, 0, 0)),
                  pl.BlockSpec((1, EXPERT_FF, d), lambda e, i: (e, 0, 0))],
        out_specs=pl.BlockSpec((1, tm, d), lambda e, i: (e, i, 0)),
        out_shape=jax.ShapeDtypeStruct((n_e, m, d), F32),
        compiler_params=_cparams(("parallel", "arbitrary")),
        name="experts",
    )(xe, gate, wgu, wd)


def _norm_kernel(x_ref, g_ref, o_ref):
    x = x_ref[0]
    ms = jnp.mean(x * x, axis=-1, keepdims=True)
    o_ref[0] = x * lax.rsqrt(ms + NORM_EPS) * g_ref[...]


def _norm_call(x, g, tm=512):
    bsz, seq, d = x.shape
    row = lambda b, i: (b, i, 0)
    return pl.pallas_call(
        _norm_kernel,
        grid=(bsz, seq // tm),
        in_specs=[pl.BlockSpec((1, tm, d), row), pl.BlockSpec((1, d), lambda b, i: (0, 0))],
        out_specs=pl.BlockSpec((1, tm, d), row),
        out_shape=jax.ShapeDtypeStruct((bsz, seq, d), F32),
        compiler_params=_cparams(("parallel", "parallel")),
        name="final_norm",
    )(x, g.reshape(1, d))


def _expert_choice(h, logits, x1, w_gate, w_up, w_down):
    bsz, seq, d = h.shape
    cap = CAPACITY_FACTOR * seq // N_EXPERTS
    affinity = jax.nn.softmax(logits[..., :N_EXPERTS], axis=-1)
    gate, idx = lax.top_k(jnp.swapaxes(affinity, 1, 2), cap)
    xe = jax.vmap(lambda hb, ib: hb[ib])(h, idx).astype(BF16)
    xe = jnp.swapaxes(xe, 0, 1).reshape(N_EXPERTS, bsz * cap, d)
    gate_e = jnp.swapaxes(gate, 0, 1).reshape(N_EXPERTS, bsz * cap, 1)
    wgu = jnp.concatenate([w_gate, w_up], axis=-1).astype(BF16)
    y = _expert_call(xe, gate_e, wgu, w_down.astype(BF16))
    y = jnp.swapaxes(y.reshape(N_EXPERTS, bsz, cap, d), 0, 1)

    def scatter(xb, yb, ib):
        return xb.at[ib.reshape(-1)].add(yb.reshape(-1, d))

    return jax.vmap(scatter)(x1, y, idx)


def kernel(x, positions, attn_norm_g, w_in, lam_q1, lam_k1, lam_q2, lam_k2, diff_subln_g, mla_q_norm_g, mla_w_uq, mla_kv_norm_g, mla_w_ukv, dil_out_g, mla_out_g, w_out, ffn_norm_g, w_router, w_gate, w_up, w_down, final_norm_g):
    depth = w_in.shape[0]
    tabs = _rope_tables(positions)
    for l in range(depth):
        lam_init = 0.8 - 0.6 * math.exp(-0.3 * l)
        w_proj, gq, wuq, gkv, wukv, wo_a, wo_b, wo_c = _prep_layer(
            w_in[l], mla_q_norm_g[l], mla_w_uq[l], mla_kv_norm_g[l], mla_w_ukv[l], w_out[l])
        qkv_a, qkv_b, qc, kc, vc = _proj_call(x, attn_norm_g[l], w_proj, tabs, gq, wuq, gkv, wukv)

        oa = _dil_call(qkv_a, dil_out_g[l])
        lam = (jnp.exp(jnp.sum(lam_q1[l] * lam_k1[l])) - jnp.exp(jnp.sum(lam_q2[l] * lam_k2[l])) + lam_init)
        lam_row = jnp.full((1, LANES), lam, F32)
        g_b = jnp.tile(diff_subln_g[l] * (1.0 - lam_init), LANES // B_V_DIM).reshape(1, LANES)
        ob = _diff_call(qkv_b, lam_row, g_b)
        g_c = jnp.pad(mla_out_g[l].reshape(C_HEADS, C_V_DIM), ((0, 0), (0, LANES - C_V_DIM))).reshape(1, C_W)
        oc = _mla_call(qc, kc, vc, g_c)

        x1, h, logits = _out_call(oa, ob, oc, x, wo_a, wo_b, wo_c, ffn_norm_g[l], w_router[l])
        x = _expert_choice(h, logits, x1, w_gate[l], w_up[l], w_down[l])
    return _norm_call(x, final_norm_g)
```

```python
import functools
import math

import jax
import jax.numpy as jnp
from jax import lax
from jax.experimental import pallas as pl
from jax.experimental.pallas import tpu as pltpu

F32 = jnp.float32
BF16 = jnp.bfloat16
LANES = 128

D_MODEL = 1024
HEAD_DIM = 64
A_HEADS = 6
A_DILATIONS = (1, 4, 16)
A_HALF_WINDOW = 64
B_HEADS = 5
B_QK_DIM = 32
B_V_DIM = 64
C_HEADS = 5
C_Q_RANK = 192
C_KV_RANK = 128
C_NOPE_DIM = 64
C_ROPE_DIM = 32
C_V_DIM = 64
C_ROPE_THETA = 10000.0
ROPE_THETA = 500000.0
ROPE_FRACTION = 4
N_EXPERTS = 16
CAPACITY_FACTOR = 2
EXPERT_FF = 1408
NORM_EPS = 1e-6
SUBLN_EPS = 1e-5
NEG_INF = -1e30

A_W = A_HEADS * HEAD_DIM
B_SLABS = 3
B_W = B_SLABS * LANES
C_W = C_HEADS * LANES
CQ_PAD = 256
PROJ_W = 3 * A_W + 3 * B_W + CQ_PAD + LANES + LANES
N_TABS = 9

VMEM_LIMIT = 56 * 1024 * 1024

_NT = (((1,), (1,)), ((), ()))


def _cparams(sem):
    return pltpu.CompilerParams(dimension_semantics=sem, vmem_limit_bytes=VMEM_LIMIT)


def _rope_tables(positions):
    pos = positions.astype(F32)

    def angles(rot_dim, theta):
        inv = 1.0 / (theta ** (jnp.arange(0, rot_dim, 2, dtype=F32) / rot_dim))
        ang = pos[:, :, None] * inv
        return jnp.cos(ang), jnp.sin(ang)

    def pattern(cos, sin, period, offset):
        half = cos.shape[-1]
        bsz, seq = cos.shape[:2]
        one = jnp.ones((bsz, seq, period), F32)
        zero = jnp.zeros((bsz, seq, period), F32)
        c = one.at[:, :, offset:offset + half].set(cos).at[:, :, offset + half:offset + 2 * half].set(cos)
        s1 = zero.at[:, :, offset:offset + half].set(-sin)
        s2 = zero.at[:, :, offset + half:offset + 2 * half].set(sin)
        reps = LANES // period
        return [jnp.tile(t, (1, 1, reps)) for t in (c, s1, s2)]

    cos_a, sin_a = angles(HEAD_DIM // ROPE_FRACTION, ROPE_THETA)
    cos_b, sin_b = angles(B_QK_DIM // ROPE_FRACTION, ROPE_THETA)
    cos_c, sin_c = angles(C_ROPE_DIM, C_ROPE_THETA)
    tabs = (pattern(cos_a, sin_a, HEAD_DIM, 0) + pattern(cos_b, sin_b, B_QK_DIM, 0)
            + pattern(cos_c, sin_c, LANES, C_NOPE_DIM))
    return jnp.concatenate(tabs, axis=-1)


def _pad_cols(w, width):
    return jnp.pad(w, ((0, 0), (0, width - w.shape[1])))


def _prep_layer(w_in, mla_q_norm_g, mla_w_uq, mla_kv_norm_g, mla_w_ukv, w_out):
    a = A_W
    bqk = B_HEADS * 2 * B_QK_DIM
    bv = B_HEADS * B_V_DIM
    o = 0
    aq = w_in[:, o:o + a] * (HEAD_DIM ** -0.5); o += a
    ak = w_in[:, o:o + a]; o += a
    av = w_in[:, o:o + a]; o += a
    bq = w_in[:, o:o + bqk] * (B_QK_DIM ** -0.5); o += bqk
    bk = w_in[:, o:o + bqk]; o += bqk
    bvv = w_in[:, o:o + bv]; o += bv
    cq = w_in[:, o:o + C_Q_RANK]; o += C_Q_RANK
    ckv = w_in[:, o:o + C_KV_RANK]; o += C_KV_RANK
    ckr = w_in[:, o:o + C_ROPE_DIM]
    ckr_slab = jnp.pad(ckr, ((0, 0), (C_NOPE_DIM, LANES - C_NOPE_DIM - C_ROPE_DIM)))
    w_proj = jnp.concatenate([aq, ak, av, _pad_cols(bq, B_W), _pad_cols(bk, B_W), _pad_cols(bvv, B_W),
                              _pad_cols(cq, CQ_PAD), ckv, ckr_slab], axis=1).astype(BF16)

    qd = C_NOPE_DIM + C_ROPE_DIM
    wuq = mla_w_uq.reshape(C_Q_RANK, C_HEADS, qd) * (qd ** -0.5)
    wuq = jnp.pad(wuq, ((0, CQ_PAD - C_Q_RANK), (0, 0), (0, LANES - qd))).reshape(CQ_PAD, C_W).astype(BF16)
    wukv = mla_w_ukv.reshape(C_KV_RANK, C_HEADS, C_NOPE_DIM + C_V_DIM)
    wk = jnp.pad(wukv[:, :, :C_NOPE_DIM], ((0, 0), (0, 0), (0, LANES - C_NOPE_DIM))).reshape(C_KV_RANK, C_W)
    wv = jnp.pad(wukv[:, :, C_NOPE_DIM:], ((0, 0), (0, 0), (0, LANES - C_V_DIM))).reshape(C_KV_RANK, C_W)
    wukv_p = jnp.concatenate([wk, wv], axis=1).astype(BF16)
    gq = jnp.pad(mla_q_norm_g, (0, CQ_PAD - C_Q_RANK)).reshape(1, CQ_PAD)
    gkv = mla_kv_norm_g.reshape(1, C_KV_RANK)

    wo_a = w_out[:a].astype(BF16)
    wo_b = jnp.pad(w_out[a:a + bv], ((0, B_W - bv), (0, 0))).astype(BF16)
    wo_c = w_out[a + bv:].reshape(C_HEADS, C_V_DIM, D_MODEL)
    wo_c = jnp.pad(wo_c, ((0, 0), (0, LANES - C_V_DIM), (0, 0))).reshape(C_W, D_MODEL).astype(BF16)
    return w_proj, gq, wuq, gkv, wukv_p, wo_a, wo_b, wo_c


def _proj_kernel(x_ref, g_ref, w_ref, tab_ref, gq_ref, wuq_ref, gkv_ref, wukv_ref,
                 qa_ref, qb_ref, qc_ref, kc_ref, vc_ref):
    x = x_ref[0]
    ms = jnp.mean(x * x, axis=-1, keepdims=True)
    h = (x * lax.rsqrt(ms + NORM_EPS) * g_ref[...]).astype(BF16)
    p = jnp.dot(h, w_ref[...], preferred_element_type=F32)

    def slab(i):
        return p[:, i * LANES:(i + 1) * LANES]

    def rope(xs, layout, shift):
        c = tab_ref[0, :, (3 * layout) * LANES:(3 * layout + 1) * LANES]
        s1 = tab_ref[0, :, (3 * layout + 1) * LANES:(3 * layout + 2) * LANES]
        s2 = tab_ref[0, :, (3 * layout + 2) * LANES:(3 * layout + 3) * LANES]
        return xs * c + pltpu.roll(xs, LANES - shift, 1) * s1 + pltpu.roll(xs, shift, 1) * s2

    n_a = A_W // LANES
    for i in range(3 * n_a):
        v = slab(i)
        if i < 2 * n_a:
            v = rope(v, 0, HEAD_DIM // ROPE_FRACTION // 2)
        qa_ref[0, :, i * LANES:(i + 1) * LANES] = v
    base = 3 * n_a
    for i in range(3 * B_SLABS):
        v = slab(base + i)
        if i < 2 * B_SLABS:
            v = rope(v, 1, B_QK_DIM // ROPE_FRACTION // 2)
        qb_ref[0, :, i * LANES:(i + 1) * LANES] = v.astype(BF16)
    base += 3 * B_SLABS

    cq = p[:, base * LANES:base * LANES + CQ_PAD]
    msq = jnp.sum(cq * cq, axis=-1, keepdims=True) * (1.0 / C_Q_RANK)
    cqn = (cq * lax.rsqrt(msq + NORM_EPS) * gq_ref[...]).astype(BF16)
    qf = jnp.dot(cqn, wuq_ref[...], preferred_element_type=F32)
    base += CQ_PAD // LANES
    ckv = slab(base)
    mskv = jnp.mean(ckv * ckv, axis=-1, keepdims=True)
    ckvn = (ckv * lax.rsqrt(mskv + NORM_EPS) * gkv_ref[...]).astype(BF16)
    kvf = jnp.dot(ckvn, wukv_ref[...], preferred_element_type=F32)
    kr = rope(slab(base + 1), 2, C_ROPE_DIM // 2)
    for hd in range(C_HEADS):
        sl = slice(hd * LANES, (hd + 1) * LANES)
        qc_ref[0, :, sl] = rope(qf[:, sl], 2, C_ROPE_DIM // 2).astype(BF16)
        kc_ref[0, :, sl] = (kvf[:, sl] + kr).astype(BF16)
        vc_ref[0, :, sl] = kvf[:, C_W + hd * LANES:C_W + (hd + 1) * LANES].astype(BF16)


def _proj_call(x, g, w_proj, tabs, gq, wuq, gkv, wukv, tm=256):
    bsz, seq, d = x.shape
    grid = (bsz, seq // tm)
    row = lambda b, i: (b, i, 0)
    const = lambda b, i: (0, 0)
    out_w = (3 * A_W, 3 * B_W, C_W, C_W, C_W)
    out_dt = (F32, BF16, BF16, BF16, BF16)
    return pl.pallas_call(
        _proj_kernel,
        grid=grid,
        in_specs=[pl.BlockSpec((1, tm, d), row),
                  pl.BlockSpec((1, d), const),
                  pl.BlockSpec((d, PROJ_W), const),
                  pl.BlockSpec((1, tm, N_TABS * LANES), row),
                  pl.BlockSpec((1, CQ_PAD), const),
                  pl.BlockSpec((CQ_PAD, C_W), const),
                  pl.BlockSpec((1, C_KV_RANK), const),
                  pl.BlockSpec((C_KV_RANK, 2 * C_W), const)],
        out_specs=[pl.BlockSpec((1, tm, w), row) for w in out_w],
        out_shape=[jax.ShapeDtypeStruct((bsz, seq, w), dt) for w, dt in zip(out_w, out_dt)],
        compiler_params=_cparams(("parallel", "parallel")),
        name="proj",
    )(x, g.reshape(1, d), w_proj, tabs, gq, wuq, gkv, wukv)


def _head_rmsnorm(o, g, eps):
    ri = lax.broadcasted_iota(jnp.int32, (LANES, LANES), 0) // HEAD_DIM
    ci = lax.broadcasted_iota(jnp.int32, (LANES, LANES), 1) // HEAD_DIM
    bd = (ri == ci).astype(F32)
    ss = jnp.dot(o * o, bd, preferred_element_type=F32, precision=lax.Precision.HIGHEST)
    return o * lax.rsqrt(ss * (1.0 / HEAD_DIM) + eps) * g


def _softmax_pv(qm, k_ref, v_ref, s_scr, tk):
    tq = qm.shape[0]
    nchunks = k_ref.shape[1] // tk
    mpart = jnp.full((tq, LANES), -jnp.inf, F32)
    for c in range(nchunks):
        s = lax.dot_general(qm, k_ref[0, c * tk:(c + 1) * tk, :], _NT, preferred_element_type=F32)
        s_scr[c] = s
        for j in range(tk // LANES):
            mpart = jnp.maximum(mpart, s[:, j * LANES:(j + 1) * LANES])
    m = jnp.max(mpart, axis=1, keepdims=True)
    lpart = jnp.zeros((tq, LANES), F32)
    acc = jnp.zeros((tq, LANES), F32)
    for c in range(nchunks):
        p = jnp.exp(s_scr[c] - m)
        for j in range(tk // LANES):
            lpart = lpart + p[:, j * LANES:(j + 1) * LANES]
        acc = acc + jnp.dot(p.astype(BF16), v_ref[0, c * tk:(c + 1) * tk, :], preferred_element_type=F32)
    l = jnp.sum(lpart, axis=1, keepdims=True)
    return acc / l


_DIL_TQ = 128
_DIL_TW = 256


def _dil_kernel(q_ref, k_ref, v_ref, g_ref, o_ref, *scr):
    seq = q_ref.shape[1]
    lane = lax.broadcasted_iota(jnp.int32, (1, LANES), 1)
    head0 = lane < HEAD_DIM
    rel = (lax.broadcasted_iota(jnp.int32, (_DIL_TQ, _DIL_TW), 1)
           - lax.broadcasted_iota(jnp.int32, (_DIL_TQ, _DIL_TW), 0))

    for pi, r in enumerate(A_DILATIONS):
        length = seq // r
        nblk = length // _DIL_TQ
        o_scr, m_scr, l_scr = scr[3 * pi:3 * pi + 3]

        def body(t, carry, r=r, length=length, nblk=nblk, o_scr=o_scr, m_scr=m_scr, l_scr=l_scr):
            c = t // nblk
            i = t % nblk
            ws = jnp.clip(i * _DIL_TQ - A_HALF_WINDOW, 0, length - _DIL_TW)
            qs = c + r * (i * _DIL_TQ)
            ks = c + r * ws
            if r == 1:
                qsl = pl.ds(pl.multiple_of(qs, _DIL_TQ), _DIL_TQ)
                ksl = pl.ds(pl.multiple_of(ks, A_HALF_WINDOW), _DIL_TW)
            else:
                qsl = pl.ds(qs, _DIL_TQ, stride=r)
                ksl = pl.ds(ks, _DIL_TW, stride=r)
            qb = q_ref[0, qsl, :]
            kb = k_ref[0, ksl, :].astype(BF16)
            vb = v_ref[0, ksl, :].astype(BF16)
            delta = rel + (ws - i * _DIL_TQ)
            valid = jnp.abs(delta) <= A_HALF_WINDOW
            res = []
            for hd in range(2):
                qm = jnp.where(head0 if hd == 0 else jnp.logical_not(head0), qb, 0.0).astype(BF16)
                s = lax.dot_general(qm, kb, _NT, preferred_element_type=F32)
                s = jnp.where(valid, s, NEG_INF)
                m = jnp.max(s, axis=1, keepdims=True)
                p = jnp.exp(s - m)
                l = jnp.sum(p, axis=1, keepdims=True)
                o = jnp.dot(p.astype(BF16), vb, preferred_element_type=F32)
                res.append((o, m, l))
            o_scr[qsl, :] = jnp.where(head0, res[0][0], res[1][0])
            m_scr[qsl, :] = jnp.where(head0, res[0][1], res[1][1])
            l_scr[qsl, :] = jnp.where(head0, res[0][2], res[1][2])
            return carry

        lax.fori_loop(0, r * nblk, body, 0, unroll=8)

    rows = 256
    g = g_ref[...]

    def combine(t, carry):
        sl = pl.ds(pl.multiple_of(t * rows, rows), rows)
        ms = [scr[3 * pi + 1][sl, :] for pi in range(3)]
        mm = jnp.maximum(jnp.maximum(ms[0], ms[1]), ms[2])
        num = jnp.zeros((rows, LANES), F32)
        den = jnp.zeros((rows, LANES), F32)
        for pi in range(3):
            e = jnp.exp(ms[pi] - mm)
            num = num + e * scr[3 * pi][sl, :]
            den = den + e * scr[3 * pi + 2][sl, :]
        o_ref[0, sl, :] = _head_rmsnorm(num / den, g, NORM_EPS).astype(o_ref.dtype)
        return carry

    lax.fori_loop(0, seq // rows, combine, 0)


def _dil_call(qkv_a, g):
    bsz, seq, _ = qkv_a.shape
    n_slab = A_W // LANES
    blk = (1, seq, LANES)
    return pl.pallas_call(
        _dil_kernel,
        grid=(bsz, n_slab),
        in_specs=[pl.BlockSpec(blk, lambda b, j: (b, 0, j)),
                  pl.BlockSpec(blk, lambda b, j: (b, 0, n_slab + j)),
                  pl.BlockSpec(blk, lambda b, j: (b, 0, 2 * n_slab + j)),
                  pl.BlockSpec((1, LANES), lambda b, j: (0, j))],
        out_specs=pl.BlockSpec(blk, lambda b, j: (b, 0, j)),
        out_shape=jax.ShapeDtypeStruct((bsz, seq, A_W), BF16),
        scratch_shapes=[pltpu.VMEM((seq, LANES), F32) for _ in range(9)],
        compiler_params=_cparams(("parallel", "parallel")),
        name="dilated_attn",
    )(qkv_a, qkv_a, qkv_a, g.reshape(1, A_W))


_ATT_TQ = 256
_ATT_TK = 512


def _diff_kernel(q_ref, k_ref, v_ref, lam_ref, g_ref, o_ref, s_scr):
    lane = lax.broadcasted_iota(jnp.int32, (1, LANES), 1)
    q = q_ref[0]
    lam = lam_ref[...]

    def head(hd):
        outs = []
        for mp in range(2):
            lo = hd * HEAD_DIM + mp * B_QK_DIM
            qm = jnp.where((lane >= lo) & (lane < lo + B_QK_DIM), q, jnp.zeros_like(q))
            outs.append(_softmax_pv(qm, k_ref, v_ref, s_scr.at[mp], _ATT_TK))
        return outs[0] - lam * outs[1]

    o_ref[0] = jnp.where(lane < HEAD_DIM, head(0), 0.0)

    @pl.when(pl.program_id(1) < B_HEADS // 2)
    def _():
        o_ref[0] = jnp.where(lane < HEAD_DIM, o_ref[0], head(1))

    o_ref[0] = _head_rmsnorm(o_ref[0], g_ref[...], SUBLN_EPS)


def _diff_call(qkv_b, lam, g):
    bsz, seq, _ = qkv_b.shape
    nq = seq // _ATT_TQ
    kv = (1, seq, LANES)
    out = pl.pallas_call(
        _diff_kernel,
        grid=(bsz, B_SLABS, nq),
        in_specs=[pl.BlockSpec((1, _ATT_TQ, LANES), lambda b, j, i: (b, i, j)),
                  pl.BlockSpec(kv, lambda b, j, i: (b, 0, B_SLABS + j)),
                  pl.BlockSpec(kv, lambda b, j, i: (b, 0, 2 * B_SLABS + j)),
                  pl.BlockSpec((1, LANES), lambda b, j, i: (0, 0)),
                  pl.BlockSpec((1, LANES), lambda b, j, i: (0, 0))],
        out_specs=pl.BlockSpec((1, _ATT_TQ, LANES), lambda b, j, i: (b, i, j)),
        out_shape=jax.ShapeDtypeStruct((bsz, seq, B_W), F32),
        scratch_shapes=[pltpu.VMEM((2, seq // _ATT_TK, _ATT_TQ, _ATT_TK), F32)],
        compiler_params=_cparams(("parallel", "parallel", "arbitrary")),
        name="diff_attn",
    )(qkv_b, qkv_b, qkv_b, lam, g)
    return out


def _mla_kernel(q_ref, k_ref, v_ref, g_ref, o_ref, s_scr):
    o = _softmax_pv(q_ref[0], k_ref, v_ref, s_scr, _ATT_TK)
    o_ref[0] = _head_rmsnorm(o, g_ref[...], NORM_EPS).astype(o_ref.dtype)


def _mla_call(qc, kc, vc, g):
    bsz, seq, _ = qc.shape
    nq = seq // _ATT_TQ
    kv = (1, seq, LANES)
    return pl.pallas_call(
        _mla_kernel,
        grid=(bsz, C_HEADS, nq),
        in_specs=[pl.BlockSpec((1, _ATT_TQ, LANES), lambda b, j, i: (b, i, j)),
                  pl.BlockSpec(kv, lambda b, j, i: (b, 0, j)),
                  pl.BlockSpec(kv, lambda b, j, i: (b, 0, j)),
                  pl.BlockSpec((1, LANES), lambda b, j, i: (0, j))],
        out_specs=pl.BlockSpec((1, _ATT_TQ, LANES), lambda b, j, i: (b, i, j)),
        out_shape=jax.ShapeDtypeStruct((bsz, seq, C_W), BF16),
        scratch_shapes=[pltpu.VMEM((seq // _ATT_TK, _ATT_TQ, _ATT_TK), F32)],
        compiler_params=_cparams(("parallel", "parallel", "arbitrary")),
        name="latent_attn",
    )(qc, kc, vc, g)


def _out_kernel(oa_ref, ob_ref, oc_ref, x_ref, wa_ref, wb_ref, wc_ref, g_ref, wr_ref,
                x1_ref, h_ref, lg_ref):
    y = jnp.dot(oa_ref[0], wa_ref[...], preferred_element_type=F32)
    y = y + jnp.dot(ob_ref[0].astype(BF16), wb_ref[...], preferred_element_type=F32)
    y = y + jnp.dot(oc_ref[0], wc_ref[...], preferred_element_type=F32)
    x1 = x_ref[0] + y
    x1_ref[0] = x1
    ms = jnp.mean(x1 * x1, axis=-1, keepdims=True)
    h = x1 * lax.rsqrt(ms + NORM_EPS) * g_ref[...]
    h_ref[0] = h.astype(h_ref.dtype)
    lg_ref[0] = jnp.dot(h, wr_ref[...], preferred_element_type=F32, precision=lax.Precision.HIGHEST)


def _out_call(oa, ob, oc, x, wo_a, wo_b, wo_c, g, w_router, tm=256):
    bsz, seq, d = x.shape
    row = lambda b, i: (b, i, 0)
    const = lambda b, i: (0, 0)
    wr = _pad_cols(w_router, LANES)
    return pl.pallas_call(
        _out_kernel,
        grid=(bsz, seq // tm),
        in_specs=[pl.BlockSpec((1, tm, A_W), row),
                  pl.BlockSpec((1, tm, B_W), row),
                  pl.BlockSpec((1, tm, C_W), row),
                  pl.BlockSpec((1, tm, d), row),
                  pl.BlockSpec((A_W, d), const),
                  pl.BlockSpec((B_W, d), const),
                  pl.BlockSpec((C_W, d), const),
                  pl.BlockSpec((1, d), const),
                  pl.BlockSpec((d, LANES), const)],
        out_specs=[pl.BlockSpec((1, tm, d), row),
                   pl.BlockSpec((1, tm, d), row),
                   pl.BlockSpec((1, tm, LANES), row)],
        out_shape=[jax.ShapeDtypeStruct((bsz, seq, d), F32),
                   jax.ShapeDtypeStruct((bsz, seq, d), BF16),
                   jax.ShapeDtypeStruct((bsz, seq, LANES), F32)],
        compiler_params=_cparams(("parallel", "parallel")),
        name="out_proj",
    )(oa, ob, oc, x, wo_a, wo_b, wo_c, g.reshape(1, d), wr)


def _router_kernel(lg_ref, pos_ref, post_ref, aff_ref, pre_scr, *, cap):
    seq = lg_ref.shape[1]
    lt = jnp.transpose(lg_ref[0])[:N_EXPERTS]
    ex = jnp.exp(lt - jnp.max(lt, axis=0, keepdims=True))
    aff = ex / jnp.sum(ex, axis=0, keepdims=True)
    aff_ref[0] = aff.reshape(N_EXPERTS, 1, seq)
    bits = pltpu.bitcast(aff, jnp.int32)

    def step(k, v):
        cand = v | lax.shift_left(jnp.int32(1), 30 - k)
        cnt = jnp.sum((bits >= cand).astype(jnp.int32), axis=1, keepdims=True)
        return jnp.where(cnt >= cap, cand, v)

    kth = lax.fori_loop(0, 31, step, jnp.zeros((N_EXPERTS, 1), jnp.int32))
    gt = bits > kth
    eq = bits == kth
    need = (cap - jnp.sum(gt.astype(jnp.int32), axis=1, keepdims=True)).astype(F32)

    flags = jnp.concatenate([gt, eq], axis=0)
    tri = (lax.broadcasted_iota(jnp.int32, (LANES, LANES), 0)
           <= lax.broadcasted_iota(jnp.int32, (LANES, LANES), 1)).astype(BF16)
    carry = jnp.zeros((2 * N_EXPERTS, 1), F32)
    for j in range(seq // LANES):
        blk = jnp.where(flags[:, j * LANES:(j + 1) * LANES], 1.0, 0.0)
        inc = jnp.dot(blk.astype(BF16), tri, preferred_element_type=F32)
        pre_scr[:, j * LANES:(j + 1) * LANES] = inc - blk + carry
        carry = carry + inc[:, LANES - 1:LANES]
    pre_gt = pre_scr[:N_EXPERTS]
    pre_eq = pre_scr[N_EXPERTS:]
    sel = jnp.logical_or(gt, jnp.logical_and(eq, pre_eq < need))
    pos = jnp.where(sel, pre_gt + jnp.minimum(pre_eq, need), -1.0)
    pos_ref[0] = pos.astype(jnp.int32).reshape(N_EXPERTS, 1, seq)
    pad = jnp.full((LANES - N_EXPERTS, seq), -1.0, F32)
    post_ref[0] = jnp.transpose(jnp.concatenate([pos, pad], axis=0))


def _router_call(logits, cap):
    bsz, seq, _ = logits.shape
    return pl.pallas_call(
        functools.partial(_router_kernel, cap=cap),
        grid=(bsz,),
        in_specs=[pl.BlockSpec((1, seq, LANES), lambda b: (b, 0, 0))],
        out_specs=[pl.BlockSpec((1, N_EXPERTS, 1, seq), lambda b: (b, 0, 0, 0)),
                   pl.BlockSpec((1, seq, LANES), lambda b: (b, 0, 0)),
                   pl.BlockSpec((1, N_EXPERTS, 1, seq), lambda b: (b, 0, 0, 0))],
        out_shape=[jax.ShapeDtypeStruct((bsz, N_EXPERTS, 1, seq), jnp.int32),
                   jax.ShapeDtypeStruct((bsz, seq, LANES), F32),
                   jax.ShapeDtypeStruct((bsz, N_EXPERTS, 1, seq), F32)],
        scratch_shapes=[pltpu.VMEM((2 * N_EXPERTS, seq), F32)],
        compiler_params=_cparams(("parallel",)),
        name="router",
    )(logits)


_GATHER_TK = 512


def _gather_kernel(pos_ref, aff_ref, h_ref, xe_ref, gate_ref, p_scr, *, cap):
    seq = h_ref.shape[1]
    slot = lax.broadcasted_iota(jnp.int32, (cap, _GATHER_TK), 0)
    gate = jnp.zeros((cap, 1), F32)
    for c in range(seq // _GATHER_TK):
        sl = slice(c * _GATHER_TK, (c + 1) * _GATHER_TK)
        hit = pos_ref[0, 0, :, sl] == slot
        p_scr[:, sl] = jnp.where(hit, 1.0, 0.0).astype(BF16)
        gate = gate + jnp.sum(jnp.where(hit, aff_ref[0, 0, :, sl], 0.0), axis=1, keepdims=True)
    xe_ref[0] = jnp.dot(p_scr[...], h_ref[0], preferred_element_type=F32).astype(BF16)
    gate_ref[0] = gate


def _gather_call(pos, aff, h_bf16, cap):
    bsz, seq, d = h_bf16.shape
    return pl.pallas_call(
        functools.partial(_gather_kernel, cap=cap),
        grid=(bsz, N_EXPERTS),
        in_specs=[pl.BlockSpec((1, 1, 1, seq), lambda b, e: (b, e, 0, 0)),
                  pl.BlockSpec((1, 1, 1, seq), lambda b, e: (b, e, 0, 0)),
                  pl.BlockSpec((1, seq, d), lambda b, e: (b, 0, 0))],
        out_specs=[pl.BlockSpec((1, cap, d), lambda b, e: (e, b, 0)),
                   pl.BlockSpec((1, cap, 1), lambda b, e: (e, b, 0))],
        out_shape=[jax.ShapeDtypeStruct((N_EXPERTS, bsz * cap, d), BF16),
                   jax.ShapeDtypeStruct((N_EXPERTS, bsz * cap, 1), F32)],
        scratch_shapes=[pltpu.VMEM((cap, seq), BF16)],
        compiler_params=_cparams(("parallel", "arbitrary")),
        name="moe_gather",
    )(pos, aff, h_bf16)


def _expert_kernel(xe_ref, gate_ref, wgu_ref, wd_ref, y_ref):
    au = jnp.dot(xe_ref[0], wgu_ref[0], preferred_element_type=F32)
    a = au[:, :EXPERT_FF]
    u = au[:, EXPERT_FF:]
    hmid = (a * jax.nn.sigmoid(a) * u).astype(BF16)
    y = jnp.dot(hmid, wd_ref[0], preferred_element_type=F32)
    y_ref[0] = (y * gate_ref[0]).astype(y_ref.dtype)


def _expert_call(xe, gate, wgu, wd, cap):
    n_e, m, d = xe.shape
    bsz = m // cap
    return pl.pallas_call(
        _expert_kernel,
        grid=(n_e, bsz),
        in_specs=[pl.BlockSpec((1, cap, d), lambda e, b: (e, b, 0)),
                  pl.BlockSpec((1, cap, 1), lambda e, b: (e, b, 0)),
                  pl.BlockSpec((1, d, 2 * EXPERT_FF), lambda e, b: (e, 0, 0)),
                  pl.BlockSpec((1, EXPERT_FF, d), lambda e, b: (e, 0, 0))],
        out_specs=pl.BlockSpec((1, cap, d), lambda e, b: (b, e, 0)),
        out_shape=jax.ShapeDtypeStruct((bsz, n_e * cap, d), BF16),
        compiler_params=_cparams(("parallel", "arbitrary")),
        name="experts",
    )(xe, gate, wgu, wd)


_SCATTER_TT = 256


def _scatter_kernel(post_ref, x1_ref, y_ref, o_ref, *, cap):
    slot = lax.broadcasted_iota(jnp.int32, (_SCATTER_TT, cap), 1).astype(F32)
    acc = x1_ref[0]
    for e in range(N_EXPERTS):
        hit = post_ref[0, :, e:e + 1] == slot
        p = jnp.where(hit, 1.0, 0.0).astype(BF16)
        acc = acc + jnp.dot(p, y_ref[0, e * cap:(e + 1) * cap, :], preferred_element_type=F32)
    o_ref[0] = acc


def _scatter_call(post, x1, y, cap, dsplit=2):
    bsz, seq, d = x1.shape
    dw = d // dsplit
    return pl.pallas_call(
        functools.partial(_scatter_kernel, cap=cap),
        grid=(bsz, dsplit, seq // _SCATTER_TT),
        in_specs=[pl.BlockSpec((1, _SCATTER_TT, LANES), lambda b, j, t: (b, t, 0)),
                  pl.BlockSpec((1, _SCATTER_TT, dw), lambda b, j, t: (b, t, j)),
                  pl.BlockSpec((1, N_EXPERTS * cap, dw), lambda b, j, t: (b, 0, j))],
        out_specs=pl.BlockSpec((1, _SCATTER_TT, dw), lambda b, j, t: (b, t, j)),
        out_shape=jax.ShapeDtypeStruct((bsz, seq, d), F32),
        compiler_params=_cparams(("parallel", "parallel", "arbitrary")),
        name="moe_scatter",
    )(post, x1, y)


def _norm_kernel(x_ref, g_ref, o_ref):
    x = x_ref[0]
    ms = jnp.mean(x * x, axis=-1, keepdims=True)
    o_ref[0] = x * lax.rsqrt(ms + NORM_EPS) * g_ref[...]


def _norm_call(x, g, tm=512):
    bsz, seq, d = x.shape
    row = lambda b, i: (b, i, 0)
    return pl.pallas_call(
        _norm_kernel,
        grid=(bsz, seq // tm),
        in_specs=[pl.BlockSpec((1, tm, d), row), pl.BlockSpec((1, d), lambda b, i: (0, 0))],
        out_specs=pl.BlockSpec((1, tm, d), row),
        out_shape=jax.ShapeDtypeStruct((bsz, seq, d), F32),
        compiler_params=_cparams(("parallel", "parallel")),
        name="final_norm",
    )(x, g.reshape(1, d))


def _expert_choice(h, logits, x1, w_gate, w_up, w_down):
    seq = h.shape[1]
    cap = CAPACITY_FACTOR * seq // N_EXPERTS
    pos, post, aff = _router_call(logits, cap)
    xe, gate = _gather_call(pos, aff, h, cap)
    wgu = jnp.concatenate([w_gate, w_up], axis=-1).astype(BF16)
    y = _expert_call(xe, gate, wgu, w_down.astype(BF16), cap)
    return _scatter_call(post, x1, y, cap)


def kernel(x, positions, attn_norm_g, w_in, lam_q1, lam_k1, lam_q2, lam_k2, diff_subln_g, mla_q_norm_g, mla_w_uq, mla_kv_norm_g, mla_w_ukv, dil_out_g, mla_out_g, w_out, ffn_norm_g, w_router, w_gate, w_up, w_down, final_norm_g):
    depth = w_in.shape[0]
    tabs = _rope_tables(positions)
    for l in range(depth):
        lam_init = 0.8 - 0.6 * math.exp(-0.3 * l)
        w_proj, gq, wuq, gkv, wukv, wo_a, wo_b, wo_c = _prep_layer(
            w_in[l], mla_q_norm_g[l], mla_w_uq[l], mla_kv_norm_g[l], mla_w_ukv[l], w_out[l])
        qkv_a, qkv_b, qc, kc, vc = _proj_call(x, attn_norm_g[l], w_proj, tabs, gq, wuq, gkv, wukv)

        oa = _dil_call(qkv_a, dil_out_g[l])
        lam = (jnp.exp(jnp.sum(lam_q1[l] * lam_k1[l])) - jnp.exp(jnp.sum(lam_q2[l] * lam_k2[l])) + lam_init)
        lam_row = jnp.full((1, LANES), lam, F32)
        g_b = jnp.tile(diff_subln_g[l] * (1.0 - lam_init), LANES // B_V_DIM).reshape(1, LANES)
        ob = _diff_call(qkv_b, lam_row, g_b)
        g_c = jnp.pad(mla_out_g[l].reshape(C_HEADS, C_V_DIM), ((0, 0), (0, LANES - C_V_DIM))).reshape(1, C_W)
        oc = _mla_call(qc, kc, vc, g_c)

        x1, h, logits = _out_call(oa, ob, oc, x, wo_a, wo_b, wo_c, ffn_norm_g[l], w_router[l])
        x = _expert_choice(h, logits, x1, w_gate[l], w_up[l], w_down[l])
    return _norm_call(x, final_norm_g)
```

```python
import functools
import math

import jax
import jax.numpy as jnp
from jax import lax
from jax.experimental import pallas as pl
from jax.experimental.pallas import tpu as pltpu

F32 = jnp.float32
BF16 = jnp.bfloat16
LANES = 128

D_MODEL = 1024
HEAD_DIM = 64
A_HEADS = 6
A_DILATIONS = (1, 4, 16)
A_HALF_WINDOW = 64
B_HEADS = 5
B_QK_DIM = 32
B_V_DIM = 64
C_HEADS = 5
C_Q_RANK = 192
C_KV_RANK = 128
C_NOPE_DIM = 64
C_ROPE_DIM = 32
C_V_DIM = 64
C_ROPE_THETA = 10000.0
ROPE_THETA = 500000.0
ROPE_FRACTION = 4
N_EXPERTS = 16
CAPACITY_FACTOR = 2
EXPERT_FF = 1408
NORM_EPS = 1e-6
SUBLN_EPS = 1e-5
NEG_INF = -1e30
LOG2E = math.log2(math.e)

A_W = A_HEADS * HEAD_DIM
B_SLABS = 3
B_W = B_SLABS * LANES
C_W = C_HEADS * LANES
CQ_PAD = 256
PROJ_W = 3 * A_W + 3 * B_W + CQ_PAD + LANES + LANES
N_TABS = 9

VMEM_LIMIT = 56 * 1024 * 1024

_NT = (((1,), (1,)), ((), ()))


def _cparams(sem):
    return pltpu.CompilerParams(dimension_semantics=sem, vmem_limit_bytes=VMEM_LIMIT)


def _rope_tables(positions):
    pos = positions.astype(F32)

    def angles(rot_dim, theta):
        inv = 1.0 / (theta ** (jnp.arange(0, rot_dim, 2, dtype=F32) / rot_dim))
        ang = pos[:, :, None] * inv
        return jnp.cos(ang), jnp.sin(ang)

    def pattern(cos, sin, period, offset):
        half = cos.shape[-1]
        bsz, seq = cos.shape[:2]
        one = jnp.ones((bsz, seq, period), F32)
        zero = jnp.zeros((bsz, seq, period), F32)
        c = one.at[:, :, offset:offset + half].set(cos).at[:, :, offset + half:offset + 2 * half].set(cos)
        s1 = zero.at[:, :, offset:offset + half].set(-sin)
        s2 = zero.at[:, :, offset + half:offset + 2 * half].set(sin)
        reps = LANES // period
        return [jnp.tile(t, (1, 1, reps)) for t in (c, s1, s2)]

    cos_a, sin_a = angles(HEAD_DIM // ROPE_FRACTION, ROPE_THETA)
    cos_b, sin_b = angles(B_QK_DIM // ROPE_FRACTION, ROPE_THETA)
    cos_c, sin_c = angles(C_ROPE_DIM, C_ROPE_THETA)
    tabs = (pattern(cos_a, sin_a, HEAD_DIM, 0) + pattern(cos_b, sin_b, B_QK_DIM, 0)
            + pattern(cos_c, sin_c, LANES, C_NOPE_DIM))
    return jnp.concatenate(tabs, axis=-1)


def _pad_cols(w, width):
    return jnp.pad(w, ((0, 0), (0, width - w.shape[1])))


def _prep_layer(w_in, mla_q_norm_g, mla_w_uq, mla_kv_norm_g, mla_w_ukv, w_out):
    a = A_W
    bqk = B_HEADS * 2 * B_QK_DIM
    bv = B_HEADS * B_V_DIM
    o = 0
    aq = w_in[:, o:o + a] * (HEAD_DIM ** -0.5); o += a
    ak = w_in[:, o:o + a]; o += a
    av = w_in[:, o:o + a]; o += a
    bq = w_in[:, o:o + bqk] * (B_QK_DIM ** -0.5 * LOG2E); o += bqk
    bk = w_in[:, o:o + bqk]; o += bqk
    bvv = w_in[:, o:o + bv]; o += bv
    cq = w_in[:, o:o + C_Q_RANK]; o += C_Q_RANK
    ckv = w_in[:, o:o + C_KV_RANK]; o += C_KV_RANK
    ckr = w_in[:, o:o + C_ROPE_DIM]
    ckr_slab = jnp.pad(ckr, ((0, 0), (C_NOPE_DIM, LANES - C_NOPE_DIM - C_ROPE_DIM)))
    w_proj = jnp.concatenate([aq, ak, av, _pad_cols(bq, B_W), _pad_cols(bk, B_W), _pad_cols(bvv, B_W),
                              _pad_cols(cq, CQ_PAD), ckv, ckr_slab], axis=1).astype(BF16)

    qd = C_NOPE_DIM + C_ROPE_DIM
    wuq = mla_w_uq.reshape(C_Q_RANK, C_HEADS, qd) * (qd ** -0.5 * LOG2E)
    wuq = jnp.pad(wuq, ((0, CQ_PAD - C_Q_RANK), (0, 0), (0, LANES - qd))).reshape(CQ_PAD, C_W).astype(BF16)
    wukv = mla_w_ukv.reshape(C_KV_RANK, C_HEADS, C_NOPE_DIM + C_V_DIM)
    wk = jnp.pad(wukv[:, :, :C_NOPE_DIM], ((0, 0), (0, 0), (0, LANES - C_NOPE_DIM))).reshape(C_KV_RANK, C_W)
    wv = jnp.pad(wukv[:, :, C_NOPE_DIM:], ((0, 0), (0, 0), (0, LANES - C_V_DIM))).reshape(C_KV_RANK, C_W)
    wukv_p = jnp.concatenate([wk, wv], axis=1).astype(BF16)
    gq = jnp.pad(mla_q_norm_g, (0, CQ_PAD - C_Q_RANK)).reshape(1, CQ_PAD)
    gkv = mla_kv_norm_g.reshape(1, C_KV_RANK)

    wo_a = w_out[:a].astype(BF16)
    wo_b = jnp.pad(w_out[a:a + bv], ((0, B_W - bv), (0, 0))).astype(BF16)
    wo_c = w_out[a + bv:].reshape(C_HEADS, C_V_DIM, D_MODEL)
    wo_c = jnp.pad(wo_c, ((0, 0), (0, LANES - C_V_DIM), (0, 0))).reshape(C_W, D_MODEL).astype(BF16)
    return w_proj, gq, wuq, gkv, wukv_p, wo_a, wo_b, wo_c


def _proj_kernel(x_ref, g_ref, w_ref, tab_ref, gq_ref, wuq_ref, gkv_ref, wukv_ref,
                 qa_ref, qb_ref, qc_ref, kc_ref, vc_ref):
    x = x_ref[0]
    ms = jnp.mean(x * x, axis=-1, keepdims=True)
    h = (x * lax.rsqrt(ms + NORM_EPS) * g_ref[...]).astype(BF16)
    p = jnp.dot(h, w_ref[...], preferred_element_type=F32)

    def slab(i):
        return p[:, i * LANES:(i + 1) * LANES]

    def rope(xs, layout, shift):
        c = tab_ref[0, :, (3 * layout) * LANES:(3 * layout + 1) * LANES]
        s1 = tab_ref[0, :, (3 * layout + 1) * LANES:(3 * layout + 2) * LANES]
        s2 = tab_ref[0, :, (3 * layout + 2) * LANES:(3 * layout + 3) * LANES]
        return xs * c + pltpu.roll(xs, LANES - shift, 1) * s1 + pltpu.roll(xs, shift, 1) * s2

    n_a = A_W // LANES
    for i in range(3 * n_a):
        v = slab(i)
        if i < 2 * n_a:
            v = rope(v, 0, HEAD_DIM // ROPE_FRACTION // 2)
        qa_ref[0, :, i * LANES:(i + 1) * LANES] = v
    base = 3 * n_a
    for i in range(3 * B_SLABS):
        v = slab(base + i)
        if i < 2 * B_SLABS:
            v = rope(v, 1, B_QK_DIM // ROPE_FRACTION // 2)
        qb_ref[0, :, i * LANES:(i + 1) * LANES] = v.astype(BF16)
    base += 3 * B_SLABS

    cq = p[:, base * LANES:base * LANES + CQ_PAD]
    msq = jnp.sum(cq * cq, axis=-1, keepdims=True) * (1.0 / C_Q_RANK)
    cqn = (cq * lax.rsqrt(msq + NORM_EPS) * gq_ref[...]).astype(BF16)
    qf = jnp.dot(cqn, wuq_ref[...], preferred_element_type=F32)
    base += CQ_PAD // LANES
    ckv = slab(base)
    mskv = jnp.mean(ckv * ckv, axis=-1, keepdims=True)
    ckvn = (ckv * lax.rsqrt(mskv + NORM_EPS) * gkv_ref[...]).astype(BF16)
    kvf = jnp.dot(ckvn, wukv_ref[...], preferred_element_type=F32)
    kr = rope(slab(base + 1), 2, C_ROPE_DIM // 2)
    for hd in range(C_HEADS):
        sl = slice(hd * LANES, (hd + 1) * LANES)
        qc_ref[0, :, sl] = rope(qf[:, sl], 2, C_ROPE_DIM // 2).astype(BF16)
        kc_ref[0, :, sl] = (kvf[:, sl] + kr).astype(BF16)
        vc_ref[0, :, sl] = kvf[:, C_W + hd * LANES:C_W + (hd + 1) * LANES].astype(BF16)


def _proj_call(x, g, w_proj, tabs, gq, wuq, gkv, wukv, tm=256):
    bsz, seq, d = x.shape
    grid = (bsz, seq // tm)
    row = lambda b, i: (b, i, 0)
    const = lambda b, i: (0, 0)
    out_w = (3 * A_W, 3 * B_W, C_W, C_W, C_W)
    out_dt = (F32, BF16, BF16, BF16, BF16)
    return pl.pallas_call(
        _proj_kernel,
        grid=grid,
        in_specs=[pl.BlockSpec((1, tm, d), row),
                  pl.BlockSpec((1, d), const),
                  pl.BlockSpec((d, PROJ_W), const),
                  pl.BlockSpec((1, tm, N_TABS * LANES), row),
                  pl.BlockSpec((1, CQ_PAD), const),
                  pl.BlockSpec((CQ_PAD, C_W), const),
                  pl.BlockSpec((1, C_KV_RANK), const),
                  pl.BlockSpec((C_KV_RANK, 2 * C_W), const)],
        out_specs=[pl.BlockSpec((1, tm, w), row) for w in out_w],
        out_shape=[jax.ShapeDtypeStruct((bsz, seq, w), dt) for w, dt in zip(out_w, out_dt)],
        compiler_params=_cparams(("parallel", "parallel")),
        name="proj",
    )(x, g.reshape(1, d), w_proj, tabs, gq, wuq, gkv, wukv)


def _head_rmsnorm_t(ot, g_t, eps):
    ss = jnp.sum(ot * ot, axis=0, keepdims=True) * (1.0 / HEAD_DIM)
    return ot * lax.rsqrt(ss + eps) * g_t


def _head_rmsnorm(o, g, eps):
    ri = lax.broadcasted_iota(jnp.int32, (LANES, LANES), 0) // HEAD_DIM
    ci = lax.broadcasted_iota(jnp.int32, (LANES, LANES), 1) // HEAD_DIM
    bd = (ri == ci).astype(F32)
    ss = jnp.dot(o * o, bd, preferred_element_type=F32, precision=lax.Precision.HIGHEST)
    return o * lax.rsqrt(ss * (1.0 / HEAD_DIM) + eps) * g


_ATT_TQ = 256
_ATT_TK = 512


def _transpose_v(v_ref, vt_scr):
    for c in range(v_ref.shape[1] // _ATT_TK):
        sl = slice(c * _ATT_TK, (c + 1) * _ATT_TK)
        vt_scr[:, sl] = jnp.transpose(v_ref[0, sl, :].astype(F32)).astype(BF16)


def _scores_phase(qm, k_ref, s_scr, m_scr, slot):
    tq = qm.shape[0]
    mx = jnp.full((8, tq), -jnp.inf, F32)
    for c in range(k_ref.shape[1] // _ATT_TK):
        s = lax.dot_general(k_ref[0, c * _ATT_TK:(c + 1) * _ATT_TK, :], qm, _NT, preferred_element_type=F32)
        s_scr[slot, c] = s
        mx = jnp.maximum(mx, jnp.max(s.reshape(_ATT_TK // 8, 8, tq), axis=0))
    m_scr[slot] = mx


def _softmax_phase(vt_scr, rows, s_scr, m_scr, slot):
    m = jnp.max(m_scr[slot], axis=0, keepdims=True)
    tq = m.shape[1]
    lp = jnp.zeros((8, tq), F32)
    acc = jnp.zeros((rows.stop - rows.start, tq), F32)
    for c in range(s_scr.shape[1]):
        p = jnp.exp2(s_scr[slot, c] - m)
        lp = lp + jnp.sum(p.reshape(_ATT_TK // 8, 8, tq), axis=0)
        acc = acc + jnp.dot(vt_scr[rows, c * _ATT_TK:(c + 1) * _ATT_TK], p.astype(BF16),
                            preferred_element_type=F32)
    return acc / jnp.sum(lp, axis=0, keepdims=True)


def _attn_jobs(jobs, q_next, first, k_ref, vt_scr, rows, s_scr, m_scr):
    assert len(jobs) % 2 == 0

    @pl.when(first)
    def _():
        _scores_phase(jobs[0], k_ref, s_scr, m_scr, 0)

    outs = []
    for n in range(len(jobs)):
        nxt = jobs[n + 1] if n + 1 < len(jobs) else q_next
        _scores_phase(nxt, k_ref, s_scr, m_scr, (n + 1) % 2)
        outs.append(_softmax_phase(vt_scr, rows[n], s_scr, m_scr, n % 2))
    return outs


_DIL_TQ = 128
_DIL_TW = 256


def _dil_kernel(q_ref, k_ref, v_ref, g_ref, o_ref, *scr):
    seq = q_ref.shape[1]
    lane = lax.broadcasted_iota(jnp.int32, (1, LANES), 1)
    head0 = lane < HEAD_DIM
    rel = (lax.broadcasted_iota(jnp.int32, (_DIL_TQ, _DIL_TW), 1)
           - lax.broadcasted_iota(jnp.int32, (_DIL_TQ, _DIL_TW), 0))

    for pi, r in enumerate(A_DILATIONS):
        length = seq // r
        nblk = length // _DIL_TQ
        o_scr, m_scr, l_scr = scr[3 * pi:3 * pi + 3]

        def body(t, carry, r=r, length=length, nblk=nblk, o_scr=o_scr, m_scr=m_scr, l_scr=l_scr):
            c = t // nblk
            i = t % nblk
            ws = jnp.clip(i * _DIL_TQ - A_HALF_WINDOW, 0, length - _DIL_TW)
            qs = c + r * (i * _DIL_TQ)
            ks = c + r * ws
            if r == 1:
                qsl = pl.ds(pl.multiple_of(qs, _DIL_TQ), _DIL_TQ)
                ksl = pl.ds(pl.multiple_of(ks, A_HALF_WINDOW), _DIL_TW)
            else:
                qsl = pl.ds(qs, _DIL_TQ, stride=r)
                ksl = pl.ds(ks, _DIL_TW, stride=r)
            qb = q_ref[0, qsl, :]
            kb = k_ref[0, ksl, :].astype(BF16)
            vb = v_ref[0, ksl, :].astype(BF16)
            delta = rel + (ws - i * _DIL_TQ)
            valid = jnp.abs(delta) <= A_HALF_WINDOW
            res = []
            for hd in range(2):
                qm = jnp.where(head0 if hd == 0 else jnp.logical_not(head0), qb, 0.0).astype(BF16)
                s = lax.dot_general(qm, kb, _NT, preferred_element_type=F32)
                s = jnp.where(valid, s, NEG_INF)
                m = jnp.max(s, axis=1, keepdims=True)
                p = jnp.exp(s - m)
                l = jnp.sum(p, axis=1, keepdims=True)
                o = jnp.dot(p.astype(BF16), vb, preferred_element_type=F32)
                res.append((o, m, l))
            o_scr[qsl, :] = jnp.where(head0, res[0][0], res[1][0])
            m_scr[qsl, :] = jnp.where(head0, res[0][1], res[1][1])
            l_scr[qsl, :] = jnp.where(head0, res[0][2], res[1][2])
            return carry

        lax.fori_loop(0, r * nblk, body, 0, unroll=8)

    rows = 256
    g = g_ref[...]

    def combine(t, carry):
        sl = pl.ds(pl.multiple_of(t * rows, rows), rows)
        ms = [scr[3 * pi + 1][sl, :] for pi in range(3)]
        mm = jnp.maximum(jnp.maximum(ms[0], ms[1]), ms[2])
        num = jnp.zeros((rows, LANES), F32)
        den = jnp.zeros((rows, LANES), F32)
        for pi in range(3):
            e = jnp.exp(ms[pi] - mm)
            num = num + e * scr[3 * pi][sl, :]
            den = den + e * scr[3 * pi + 2][sl, :]
        o_ref[0, sl, :] = _head_rmsnorm(num / den, g, NORM_EPS).astype(o_ref.dtype)
        return carry

    lax.fori_loop(0, seq // rows, combine, 0)


def _dil_call(qkv_a, g):
    bsz, seq, _ = qkv_a.shape
    n_slab = A_W // LANES
    blk = (1, seq, LANES)
    return pl.pallas_call(
        _dil_kernel,
        grid=(bsz, n_slab),
        in_specs=[pl.BlockSpec(blk, lambda b, j: (b, 0, j)),
                  pl.BlockSpec(blk, lambda b, j: (b, 0, n_slab + j)),
                  pl.BlockSpec(blk, lambda b, j: (b, 0, 2 * n_slab + j)),
                  pl.BlockSpec((1, LANES), lambda b, j: (0, j))],
        out_specs=pl.BlockSpec(blk, lambda b, j: (b, 0, j)),
        out_shape=jax.ShapeDtypeStruct((bsz, seq, A_W), BF16),
        scratch_shapes=[pltpu.VMEM((seq, LANES), F32) for _ in range(9)],
        compiler_params=_cparams(("parallel", "parallel")),
        name="dilated_attn",
    )(qkv_a, qkv_a, qkv_a, g.reshape(1, A_W))


def _attn_scratch(seq):
    return [pltpu.VMEM((LANES, seq), BF16),
            pltpu.VMEM((2, seq // _ATT_TK, _ATT_TK, _ATT_TQ), F32),
            pltpu.VMEM((2, 8, _ATT_TQ), F32)]


_DIFF_TILES = 2


def _diff_kernel(lam_ref, q_ref, qn_ref, k_ref, v_ref, g_ref, o_ref, vt_scr, s_scr, m_scr):
    first = pl.program_id(2) == 0
    lane = lax.broadcasted_iota(jnp.int32, (1, LANES), 1)
    lam = lam_ref[0]

    def qmap(q, mp):
        lo = mp * B_QK_DIM
        return jnp.where((lane >= lo) & (lane < lo + B_QK_DIM), q, jnp.zeros_like(q))

    @pl.when(first)
    def _():
        _transpose_v(v_ref, vt_scr)

    def run(n_heads):
        nm = 2 * n_heads
        jobs = [qmap(q_ref[0, t * _ATT_TQ:(t + 1) * _ATT_TQ, :], mp)
                for t in range(_DIFF_TILES) for mp in range(nm)]
        rows = [slice((mp // 2) * HEAD_DIM, (mp // 2 + 1) * HEAD_DIM) for mp in range(nm)] * _DIFF_TILES
        outs = _attn_jobs(jobs, qmap(qn_ref[0], 0), first, k_ref, vt_scr, rows, s_scr, m_scr)
        for t in range(_DIFF_TILES):
            heads = [_head_rmsnorm_t(outs[t * nm + 2 * hd] - lam * outs[t * nm + 2 * hd + 1],
                                     g_ref[hd * HEAD_DIM:(hd + 1) * HEAD_DIM, :], SUBLN_EPS)
                     for hd in range(n_heads)]
            if n_heads == 1:
                heads.append(jnp.zeros_like(heads[0]))
            o_ref[0, t * _ATT_TQ:(t + 1) * _ATT_TQ, :] = jnp.transpose(
                jnp.concatenate(heads, axis=0)).astype(o_ref.dtype)

    @pl.when(pl.program_id(1) < B_HEADS // 2)
    def _():
        run(2)

    @pl.when(pl.program_id(1) == B_HEADS // 2)
    def _():
        run(1)


def _diff_call(qkv_b, lam, g):
    bsz, seq, _ = qkv_b.shape
    nq = seq // (_DIFF_TILES * _ATT_TQ)
    kv = (1, seq, LANES)
    last = seq // _ATT_TQ - 1
    return pl.pallas_call(
        _diff_kernel,
        grid=(bsz, B_SLABS, nq),
        in_specs=[pl.BlockSpec(memory_space=pltpu.SMEM),
                  pl.BlockSpec((1, _DIFF_TILES * _ATT_TQ, LANES), lambda b, j, i: (b, i, j)),
                  pl.BlockSpec((1, _ATT_TQ, LANES),
                               lambda b, j, i: (b, jnp.minimum(_DIFF_TILES * (i + 1), last), j)),
                  pl.BlockSpec(kv, lambda b, j, i: (b, 0, B_SLABS + j)),
                  pl.BlockSpec(kv, lambda b, j, i: (b, 0, 2 * B_SLABS + j)),
                  pl.BlockSpec((LANES, _ATT_TQ), lambda b, j, i: (0, 0))],
        out_specs=pl.BlockSpec((1, _DIFF_TILES * _ATT_TQ, LANES), lambda b, j, i: (b, i, j)),
        out_shape=jax.ShapeDtypeStruct((bsz, seq, B_W), BF16),
        scratch_shapes=_attn_scratch(seq),
        compiler_params=_cparams(("parallel", "parallel", "arbitrary")),
        name="diff_attn",
    )(lam, qkv_b, qkv_b, qkv_b, qkv_b, g)


_MLA_TILES = 4


def _mla_kernel(q_ref, qn_ref, k_ref, v_ref, g_ref, o_ref, vt_scr, s_scr, m_scr):
    first = pl.program_id(2) == 0

    @pl.when(first)
    def _():
        _transpose_v(v_ref, vt_scr)

    jobs = [q_ref[0, n * _ATT_TQ:(n + 1) * _ATT_TQ, :] for n in range(_MLA_TILES)]
    rows = [slice(0, C_V_DIM)] * _MLA_TILES
    outs = _attn_jobs(jobs, qn_ref[0], first, k_ref, vt_scr, rows, s_scr, m_scr)
    for n, o in enumerate(outs):
        on = _head_rmsnorm_t(o, g_ref[:C_V_DIM, :], NORM_EPS)
        ot = jnp.concatenate([on, jnp.zeros_like(on)], axis=0)
        o_ref[0, n * _ATT_TQ:(n + 1) * _ATT_TQ, :] = jnp.transpose(ot).astype(o_ref.dtype)


def _mla_call(qc, kc, vc, g):
    bsz, seq, _ = qc.shape
    nq = seq // (_MLA_TILES * _ATT_TQ)
    kv = (1, seq, LANES)
    last = seq // _ATT_TQ - 1
    return pl.pallas_call(
        _mla_kernel,
        grid=(bsz, C_HEADS, nq),
        in_specs=[pl.BlockSpec((1, _MLA_TILES * _ATT_TQ, LANES), lambda b, j, i: (b, i, j)),
                  pl.BlockSpec((1, _ATT_TQ, LANES),
                               lambda b, j, i: (b, jnp.minimum(_MLA_TILES * (i + 1), last), j)),
                  pl.BlockSpec(kv, lambda b, j, i: (b, 0, j)),
                  pl.BlockSpec(kv, lambda b, j, i: (b, 0, j)),
                  pl.BlockSpec((LANES, _ATT_TQ), lambda b, j, i: (j, 0))],
        out_specs=pl.BlockSpec((1, _MLA_TILES * _ATT_TQ, LANES), lambda b, j, i: (b, i, j)),
        out_shape=jax.ShapeDtypeStruct((bsz, seq, C_W), BF16),
        scratch_shapes=_attn_scratch(seq),
        compiler_params=_cparams(("parallel", "parallel", "arbitrary")),
        name="latent_attn",
    )(qc, qc, kc, vc, g)


def _out_kernel(oa_ref, ob_ref, oc_ref, x_ref, wa_ref, wb_ref, wc_ref, g_ref, wr_ref,
                x1_ref, h_ref, lg_ref):
    y = jnp.dot(oa_ref[0], wa_ref[...], preferred_element_type=F32)
    y = y + jnp.dot(ob_ref[0], wb_ref[...], preferred_element_type=F32)
    y = y + jnp.dot(oc_ref[0], wc_ref[...], preferred_element_type=F32)
    x1 = x_ref[0] + y
    x1_ref[0] = x1
    ms = jnp.mean(x1 * x1, axis=-1, keepdims=True)
    h = x1 * lax.rsqrt(ms + NORM_EPS) * g_ref[...]
    h_ref[0] = h.astype(h_ref.dtype)
    lg_ref[0] = jnp.dot(h, wr_ref[...], preferred_element_type=F32, precision=lax.Precision.HIGHEST)


def _out_call(oa, ob, oc, x, wo_a, wo_b, wo_c, g, w_router, tm=256):
    bsz, seq, d = x.shape
    row = lambda b, i: (b, i, 0)
    const = lambda b, i: (0, 0)
    wr = _pad_cols(w_router, LANES)
    return pl.pallas_call(
        _out_kernel,
        grid=(bsz, seq // tm),
        in_specs=[pl.BlockSpec((1, tm, A_W), row),
                  pl.BlockSpec((1, tm, B_W), row),
                  pl.BlockSpec((1, tm, C_W), row),
                  pl.BlockSpec((1, tm, d), row),
                  pl.BlockSpec((A_W, d), const),
                  pl.BlockSpec((B_W, d), const),
                  pl.BlockSpec((C_W, d), const),
                  pl.BlockSpec((1, d), const),
                  pl.BlockSpec((d, LANES), const)],
        out_specs=[pl.BlockSpec((1, tm, d), row),
                   pl.BlockSpec((1, tm, d), row),
                   pl.BlockSpec((1, tm, LANES), row)],
        out_shape=[jax.ShapeDtypeStruct((bsz, seq, d), F32),
                   jax.ShapeDtypeStruct((bsz, seq, d), BF16),
                   jax.ShapeDtypeStruct((bsz, seq, LANES), F32)],
        compiler_params=_cparams(("parallel", "parallel")),
        name="out_proj",
    )(oa, ob, oc, x, wo_a, wo_b, wo_c, g.reshape(1, d), wr)


def _router_kernel(lg_ref, pos_ref, post_ref, aff_ref, pre_scr, *, cap):
    seq = lg_ref.shape[1]
    lt = jnp.transpose(lg_ref[0])[:N_EXPERTS]
    ex = jnp.exp(lt - jnp.max(lt, axis=0, keepdims=True))
    aff = ex / jnp.sum(ex, axis=0, keepdims=True)
    aff_ref[0] = aff.reshape(N_EXPERTS, 1, seq)
    bits = pltpu.bitcast(aff, jnp.int32)

    def step(k, v):
        cand = v | lax.shift_left(jnp.int32(1), 30 - k)
        cnt = jnp.sum((bits >= cand).astype(jnp.int32), axis=1, keepdims=True)
        return jnp.where(cnt >= cap, cand, v)

    kth = lax.fori_loop(0, 31, step, jnp.zeros((N_EXPERTS, 1), jnp.int32))
    gt = bits > kth
    eq = bits == kth
    need = (cap - jnp.sum(gt.astype(jnp.int32), axis=1, keepdims=True)).astype(F32)

    flags = jnp.concatenate([gt, eq], axis=0)
    tri = (lax.broadcasted_iota(jnp.int32, (LANES, LANES), 0)
           <= lax.broadcasted_iota(jnp.int32, (LANES, LANES), 1)).astype(BF16)
    carry = jnp.zeros((2 * N_EXPERTS, 1), F32)
    for j in range(seq // LANES):
        blk = jnp.where(flags[:, j * LANES:(j + 1) * LANES], 1.0, 0.0)
        inc = jnp.dot(blk.astype(BF16), tri, preferred_element_type=F32)
        pre_scr[:, j * LANES:(j + 1) * LANES] = inc - blk + carry
        carry = carry + inc[:, LANES - 1:LANES]
    pre_gt = pre_scr[:N_EXPERTS]
    pre_eq = pre_scr[N_EXPERTS:]
    sel = jnp.logical_or(gt, jnp.logical_and(eq, pre_eq < need))
    pos = jnp.where(sel, pre_gt + jnp.minimum(pre_eq, need), -1.0)
    pos_ref[0] = pos.astype(jnp.int32).reshape(N_EXPERTS, 1, seq)
    pad = jnp.full((LANES - N_EXPERTS, seq), -1.0, F32)
    post_ref[0] = jnp.transpose(jnp.concatenate([pos, pad], axis=0))


def _router_call(logits, cap):
    bsz, seq, _ = logits.shape
    return pl.pallas_call(
        functools.partial(_router_kernel, cap=cap),
        grid=(bsz,),
        in_specs=[pl.BlockSpec((1, seq, LANES), lambda b: (b, 0, 0))],
        out_specs=[pl.BlockSpec((1, N_EXPERTS, 1, seq), lambda b: (b, 0, 0, 0)),
                   pl.BlockSpec((1, seq, LANES), lambda b: (b, 0, 0)),
                   pl.BlockSpec((1, N_EXPERTS, 1, seq), lambda b: (b, 0, 0, 0))],
        out_shape=[jax.ShapeDtypeStruct((bsz, N_EXPERTS, 1, seq), jnp.int32),
                   jax.ShapeDtypeStruct((bsz, seq, LANES), F32),
                   jax.ShapeDtypeStruct((bsz, N_EXPERTS, 1, seq), F32)],
        scratch_shapes=[pltpu.VMEM((2 * N_EXPERTS, seq), F32)],
        compiler_params=_cparams(("parallel",)),
        name="router",
    )(logits)


_GATHER_TK = 512


def _gather_kernel(pos_ref, aff_ref, h_ref, xe_ref, gate_ref, p_scr, *, cap):
    seq = h_ref.shape[1]
    slot = lax.broadcasted_iota(jnp.int32, (cap, _GATHER_TK), 0)
    gate = jnp.zeros((cap, 1), F32)
    for c in range(seq // _GATHER_TK):
        sl = slice(c * _GATHER_TK, (c + 1) * _GATHER_TK)
        hit = pos_ref[0, 0, :, sl] == slot
        p_scr[:, sl] = jnp.where(hit, 1.0, 0.0).astype(BF16)
        gate = gate + jnp.sum(jnp.where(hit, aff_ref[0, 0, :, sl], 0.0), axis=1, keepdims=True)
    xe_ref[0] = jnp.dot(p_scr[...], h_ref[0], preferred_element_type=F32).astype(BF16)
    gate_ref[0] = gate


def _gather_call(pos, aff, h_bf16, cap):
    bsz, seq, d = h_bf16.shape
    return pl.pallas_call(
        functools.partial(_gather_kernel, cap=cap),
        grid=(bsz, N_EXPERTS),
        in_specs=[pl.BlockSpec((1, 1, 1, seq), lambda b, e: (b, e, 0, 0)),
                  pl.BlockSpec((1, 1, 1, seq), lambda b, e: (b, e, 0, 0)),
                  pl.BlockSpec((1, seq, d), lambda b, e: (b, 0, 0))],
        out_specs=[pl.BlockSpec((1, cap, d), lambda b, e: (e, b, 0)),
                   pl.BlockSpec((1, cap, 1), lambda b, e: (e, b, 0))],
        out_shape=[jax.ShapeDtypeStruct((N_EXPERTS, bsz * cap, d), BF16),
                   jax.ShapeDtypeStruct((N_EXPERTS, bsz * cap, 1), F32)],
        scratch_shapes=[pltpu.VMEM((cap, seq), BF16)],
        compiler_params=_cparams(("parallel", "arbitrary")),
        name="moe_gather",
    )(pos, aff, h_bf16)


def _expert_kernel(xe_ref, gate_ref, wgu_ref, wd_ref, y_ref):
    au = jnp.dot(xe_ref[0], wgu_ref[0], preferred_element_type=F32)
    a = au[:, :EXPERT_FF]
    u = au[:, EXPERT_FF:]
    hmid = (a * jax.nn.sigmoid(a) * u).astype(BF16)
    y = jnp.dot(hmid, wd_ref[0], preferred_element_type=F32)
    y_ref[0] = (y * gate_ref[0]).astype(y_ref.dtype)


def _expert_call(xe, gate, wgu, wd, cap):
    n_e, m, d = xe.shape
    bsz = m // cap
    return pl.pallas_call(
        _expert_kernel,
        grid=(n_e, bsz),
        in_specs=[pl.BlockSpec((1, cap, d), lambda e, b: (e, b, 0)),
                  pl.BlockSpec((1, cap, 1), lambda e, b: (e, b, 0)),
                  pl.BlockSpec((1, d, 2 * EXPERT_FF), lambda e, b: (e, 0, 0)),
                  pl.BlockSpec((1, EXPERT_FF, d), lambda e, b: (e, 0, 0))],
        out_specs=pl.BlockSpec((1, cap, d), lambda e, b: (b, e, 0)),
        out_shape=jax.ShapeDtypeStruct((bsz, n_e * cap, d), BF16),
        compiler_params=_cparams(("parallel", "arbitrary")),
        name="experts",
    )(xe, gate, wgu, wd)


_SCATTER_TT = 256


def _scatter_kernel(post_ref, x1_ref, y_ref, o_ref, *, cap):
    slot = lax.broadcasted_iota(jnp.int32, (_SCATTER_TT, cap), 1).astype(F32)
    acc = x1_ref[0]
    for e in range(N_EXPERTS):
        hit = post_ref[0, :, e:e + 1] == slot
        p = jnp.where(hit, 1.0, 0.0).astype(BF16)
        acc = acc + jnp.dot(p, y_ref[0, e * cap:(e + 1) * cap, :], preferred_element_type=F32)
    o_ref[0] = acc


def _scatter_call(post, x1, y, cap, dsplit=2):
    bsz, seq, d = x1.shape
    dw = d // dsplit
    return pl.pallas_call(
        functools.partial(_scatter_kernel, cap=cap),
        grid=(bsz, dsplit, seq // _SCATTER_TT),
        in_specs=[pl.BlockSpec((1, _SCATTER_TT, LANES), lambda b, j, t: (b, t, 0)),
                  pl.BlockSpec((1, _SCATTER_TT, dw), lambda b, j, t: (b, t, j)),
                  pl.BlockSpec((1, N_EXPERTS * cap, dw), lambda b, j, t: (b, 0, j))],
        out_specs=pl.BlockSpec((1, _SCATTER_TT, dw), lambda b, j, t: (b, t, j)),
        out_shape=jax.ShapeDtypeStruct((bsz, seq, d), F32),
        compiler_params=_cparams(("parallel", "parallel", "arbitrary")),
        name="moe_scatter",
    )(post, x1, y)


def _norm_kernel(x_ref, g_ref, o_ref):
    x = x_ref[0]
    ms = jnp.mean(x * x, axis=-1, keepdims=True)
    o_ref[0] = x * lax.rsqrt(ms + NORM_EPS) * g_ref[...]


def _norm_call(x, g, tm=512):
    bsz, seq, d = x.shape
    row = lambda b, i: (b, i, 0)
    return pl.pallas_call(
        _norm_kernel,
        grid=(bsz, seq // tm),
        in_specs=[pl.BlockSpec((1, tm, d), row), pl.BlockSpec((1, d), lambda b, i: (0, 0))],
        out_specs=pl.BlockSpec((1, tm, d), row),
        out_shape=jax.ShapeDtypeStruct((bsz, seq, d), F32),
        compiler_params=_cparams(("parallel", "parallel")),
        name="final_norm",
    )(x, g.reshape(1, d))


def _expert_choice(h, logits, x1, w_gate, w_up, w_down):
    seq = h.shape[1]
    cap = CAPACITY_FACTOR * seq // N_EXPERTS
    pos, post, aff = _router_call(logits, cap)
    xe, gate = _gather_call(pos, aff, h, cap)
    wgu = jnp.concatenate([w_gate, w_up], axis=-1).astype(BF16)
    y = _expert_call(xe, gate, wgu, w_down.astype(BF16), cap)
    return _scatter_call(post, x1, y, cap)


def kernel(x, positions, attn_norm_g, w_in, lam_q1, lam_k1, lam_q2, lam_k2, diff_subln_g, mla_q_norm_g, mla_w_uq, mla_kv_norm_g, mla_w_ukv, dil_out_g, mla_out_g, w_out, ffn_norm_g, w_router, w_gate, w_up, w_down, final_norm_g):
    depth = w_in.shape[0]
    tabs = _rope_tables(positions)
    for l in range(depth):
        lam_init = 0.8 - 0.6 * math.exp(-0.3 * l)
        w_proj, gq, wuq, gkv, wukv, wo_a, wo_b, wo_c = _prep_layer(
            w_in[l], mla_q_norm_g[l], mla_w_uq[l], mla_kv_norm_g[l], mla_w_ukv[l], w_out[l])
        qkv_a, qkv_b, qc, kc, vc = _proj_call(x, attn_norm_g[l], w_proj, tabs, gq, wuq, gkv, wukv)

        oa = _dil_call(qkv_a, dil_out_g[l])
        lam = (jnp.exp(jnp.sum(lam_q1[l] * lam_k1[l])) - jnp.exp(jnp.sum(lam_q2[l] * lam_k2[l])) + lam_init)
        g_b = jnp.tile(diff_subln_g[l] * (1.0 - lam_init), LANES // B_V_DIM)
        g_b = jnp.broadcast_to(g_b[:, None], (LANES, _ATT_TQ))
        ob = _diff_call(qkv_b, lam.reshape(1), g_b)
        g_c = jnp.pad(mla_out_g[l].reshape(C_HEADS, C_V_DIM), ((0, 0), (0, LANES - C_V_DIM))).reshape(C_W)
        g_c = jnp.broadcast_to(g_c[:, None], (C_W, _ATT_TQ))
        oc = _mla_call(qc, kc, vc, g_c)

        x1, h, logits = _out_call(oa, ob, oc, x, wo_a, wo_b, wo_c, ffn_norm_g[l], w_router[l])
        x = _expert_choice(h, logits, x1, w_gate[l], w_up[l], w_down[l])
    return _norm_call(x, final_norm_g)
```

```python
import functools
import math

import jax
import jax.numpy as jnp
from jax import lax
from jax.experimental import pallas as pl
from jax.experimental.pallas import tpu as pltpu

F32 = jnp.float32
BF16 = jnp.bfloat16
LANES = 128

D_MODEL = 1024
HEAD_DIM = 64
A_HEADS = 6
A_DILATIONS = (1, 4, 16)
A_HALF_WINDOW = 64
B_HEADS = 5
B_QK_DIM = 32
B_V_DIM = 64
C_HEADS = 5
C_Q_RANK = 192
C_KV_RANK = 128
C_NOPE_DIM = 64
C_ROPE_DIM = 32
C_V_DIM = 64
C_ROPE_THETA = 10000.0
ROPE_THETA = 500000.0
ROPE_FRACTION = 4
N_EXPERTS = 16
CAPACITY_FACTOR = 2
EXPERT_FF = 1408
NORM_EPS = 1e-6
SUBLN_EPS = 1e-5
NEG_INF = -1e30
LOG2E = math.log2(math.e)

A_W = A_HEADS * HEAD_DIM
B_SLABS = 3
B_W = B_SLABS * LANES
C_W = C_HEADS * LANES
CQ_PAD = 256
PROJ_W = 3 * A_W + 3 * B_W + CQ_PAD + LANES + LANES
N_TABS = 9

VMEM_LIMIT = 56 * 1024 * 1024

_NT = (((1,), (1,)), ((), ()))


def _cparams(sem):
    return pltpu.CompilerParams(dimension_semantics=sem, vmem_limit_bytes=VMEM_LIMIT)


def _rope_tables(positions):
    pos = positions.astype(F32)

    def angles(rot_dim, theta):
        inv = 1.0 / (theta ** (jnp.arange(0, rot_dim, 2, dtype=F32) / rot_dim))
        ang = pos[:, :, None] * inv
        return jnp.cos(ang), jnp.sin(ang)

    def pattern(cos, sin, period, offset):
        half = cos.shape[-1]
        bsz, seq = cos.shape[:2]
        def fill(value, n):
            return jnp.full((bsz, seq, n), value, F32)

        rest = period - offset - 2 * half
        c = jnp.concatenate([fill(1.0, offset), cos, cos, fill(1.0, rest)], axis=-1)
        s1 = jnp.concatenate([fill(0.0, offset), -sin, fill(0.0, half + rest)], axis=-1)
        s2 = jnp.concatenate([fill(0.0, offset + half), sin, fill(0.0, rest)], axis=-1)
        reps = LANES // period
        return [jnp.tile(t, (1, 1, reps)) for t in (c, s1, s2)]

    cos_a, sin_a = angles(HEAD_DIM // ROPE_FRACTION, ROPE_THETA)
    cos_b, sin_b = angles(B_QK_DIM // ROPE_FRACTION, ROPE_THETA)
    cos_c, sin_c = angles(C_ROPE_DIM, C_ROPE_THETA)
    tabs = (pattern(cos_a, sin_a, HEAD_DIM, 0) + pattern(cos_b, sin_b, B_QK_DIM, 0)
            + pattern(cos_c, sin_c, LANES, C_NOPE_DIM))
    return jnp.concatenate(tabs, axis=-1)


def _pad_cols(w, width):
    return jnp.pad(w, ((0, 0), (0, width - w.shape[1])))


def _prep_layer(w_in, mla_q_norm_g, mla_w_uq, mla_kv_norm_g, mla_w_ukv, w_out):
    a = A_W
    bqk = B_HEADS * 2 * B_QK_DIM
    bv = B_HEADS * B_V_DIM
    o = 0
    aq = w_in[:, o:o + a] * (HEAD_DIM ** -0.5); o += a
    ak = w_in[:, o:o + a]; o += a
    av = w_in[:, o:o + a]; o += a
    bq = w_in[:, o:o + bqk] * (B_QK_DIM ** -0.5 * LOG2E); o += bqk
    bk = w_in[:, o:o + bqk]; o += bqk
    bvv = w_in[:, o:o + bv]; o += bv
    cq = w_in[:, o:o + C_Q_RANK]; o += C_Q_RANK
    ckv = w_in[:, o:o + C_KV_RANK]; o += C_KV_RANK
    ckr = w_in[:, o:o + C_ROPE_DIM]
    ckr_slab = jnp.pad(ckr, ((0, 0), (C_NOPE_DIM, LANES - C_NOPE_DIM - C_ROPE_DIM)))
    w_proj = jnp.concatenate([aq, ak, av, _pad_cols(bq, B_W), _pad_cols(bk, B_W), _pad_cols(bvv, B_W),
                              _pad_cols(cq, CQ_PAD), ckv, ckr_slab], axis=1).astype(BF16)

    qd = C_NOPE_DIM + C_ROPE_DIM
    wuq = mla_w_uq.reshape(C_Q_RANK, C_HEADS, qd) * (qd ** -0.5 * LOG2E)
    wuq = jnp.pad(wuq, ((0, CQ_PAD - C_Q_RANK), (0, 0), (0, LANES - qd))).reshape(CQ_PAD, C_W).astype(BF16)
    wukv = mla_w_ukv.reshape(C_KV_RANK, C_HEADS, C_NOPE_DIM + C_V_DIM)
    wk = jnp.pad(wukv[:, :, :C_NOPE_DIM], ((0, 0), (0, 0), (0, LANES - C_NOPE_DIM))).reshape(C_KV_RANK, C_W)
    wv = jnp.pad(wukv[:, :, C_NOPE_DIM:], ((0, 0), (0, 0), (0, LANES - C_V_DIM))).reshape(C_KV_RANK, C_W)
    wukv_p = jnp.concatenate([wk, wv], axis=1).astype(BF16)
    gq = jnp.pad(mla_q_norm_g, (0, CQ_PAD - C_Q_RANK)).reshape(1, CQ_PAD)
    gkv = mla_kv_norm_g.reshape(1, C_KV_RANK)

    wo_a = w_out[:a].astype(BF16)
    wo_b = jnp.pad(w_out[a:a + bv], ((0, B_W - bv), (0, 0))).astype(BF16)
    wo_c = w_out[a + bv:].reshape(C_HEADS, C_V_DIM, D_MODEL)
    wo_c = jnp.pad(wo_c, ((0, 0), (0, LANES - C_V_DIM), (0, 0))).reshape(C_W, D_MODEL).astype(BF16)
    return w_proj, gq, wuq, gkv, wukv_p, wo_a, wo_b, wo_c


def _proj_kernel(x_ref, g_ref, w_ref, tab_ref, gq_ref, wuq_ref, gkv_ref, wukv_ref,
                 qa_ref, qb_ref, qc_ref, kc_ref, vc_ref):
    x = x_ref[0]
    ms = jnp.mean(x * x, axis=-1, keepdims=True)
    h = (x * lax.rsqrt(ms + NORM_EPS) * g_ref[...]).astype(BF16)
    p = jnp.dot(h, w_ref[...], preferred_element_type=F32)

    def slab(i):
        return p[:, i * LANES:(i + 1) * LANES]

    def rope(xs, layout, shift):
        c = tab_ref[0, :, (3 * layout) * LANES:(3 * layout + 1) * LANES]
        s1 = tab_ref[0, :, (3 * layout + 1) * LANES:(3 * layout + 2) * LANES]
        s2 = tab_ref[0, :, (3 * layout + 2) * LANES:(3 * layout + 3) * LANES]
        return xs * c + pltpu.roll(xs, LANES - shift, 1) * s1 + pltpu.roll(xs, shift, 1) * s2

    n_a = A_W // LANES
    for i in range(3 * n_a):
        v = slab(i)
        if i < 2 * n_a:
            v = rope(v, 0, HEAD_DIM // ROPE_FRACTION // 2)
        qa_ref[0, :, i * LANES:(i + 1) * LANES] = v
    base = 3 * n_a
    for i in range(3 * B_SLABS):
        v = slab(base + i)
        if i < 2 * B_SLABS:
            v = rope(v, 1, B_QK_DIM // ROPE_FRACTION // 2)
        qb_ref[0, :, i * LANES:(i + 1) * LANES] = v.astype(BF16)
    base += 3 * B_SLABS

    cq = p[:, base * LANES:base * LANES + CQ_PAD]
    msq = jnp.sum(cq * cq, axis=-1, keepdims=True) * (1.0 / C_Q_RANK)
    cqn = (cq * lax.rsqrt(msq + NORM_EPS) * gq_ref[...]).astype(BF16)
    qf = jnp.dot(cqn, wuq_ref[...], preferred_element_type=F32)
    base += CQ_PAD // LANES
    ckv = slab(base)
    mskv = jnp.mean(ckv * ckv, axis=-1, keepdims=True)
    ckvn = (ckv * lax.rsqrt(mskv + NORM_EPS) * gkv_ref[...]).astype(BF16)
    kvf = jnp.dot(ckvn, wukv_ref[...], preferred_element_type=F32)
    kr = rope(slab(base + 1), 2, C_ROPE_DIM // 2)
    for hd in range(C_HEADS):
        sl = slice(hd * LANES, (hd + 1) * LANES)
        qc_ref[0, :, sl] = rope(qf[:, sl], 2, C_ROPE_DIM // 2).astype(BF16)
        kc_ref[0, :, sl] = (kvf[:, sl] + kr).astype(BF16)
        vc_ref[0, :, sl] = kvf[:, C_W + hd * LANES:C_W + (hd + 1) * LANES].astype(BF16)


def _proj_call(x, g, w_proj, tabs, gq, wuq, gkv, wukv, tm=512):
    bsz, seq, d = x.shape
    grid = (bsz, seq // tm)
    row = lambda b, i: (b, i, 0)
    const = lambda b, i: (0, 0)
    out_w = (3 * A_W, 3 * B_W, C_W, C_W, C_W)
    out_dt = (F32, BF16, BF16, BF16, BF16)
    return pl.pallas_call(
        _proj_kernel,
        grid=grid,
        in_specs=[pl.BlockSpec((1, tm, d), row),
                  pl.BlockSpec((1, d), const),
                  pl.BlockSpec((d, PROJ_W), const),
                  pl.BlockSpec((1, tm, N_TABS * LANES), row),
                  pl.BlockSpec((1, CQ_PAD), const),
                  pl.BlockSpec((CQ_PAD, C_W), const),
                  pl.BlockSpec((1, C_KV_RANK), const),
                  pl.BlockSpec((C_KV_RANK, 2 * C_W), const)],
        out_specs=[pl.BlockSpec((1, tm, w), row) for w in out_w],
        out_shape=[jax.ShapeDtypeStruct((bsz, seq, w), dt) for w, dt in zip(out_w, out_dt)],
        compiler_params=_cparams(("parallel", "parallel")),
        name="proj",
    )(x, g.reshape(1, d), w_proj, tabs, gq, wuq, gkv, wukv)


def _head_rmsnorm_t(ot, g_t, eps):
    ss = jnp.sum(ot * ot, axis=0, keepdims=True) * (1.0 / HEAD_DIM)
    return ot * lax.rsqrt(ss + eps) * g_t


def _head_rmsnorm(o, g, eps):
    ri = lax.broadcasted_iota(jnp.int32, (LANES, LANES), 0) // HEAD_DIM
    ci = lax.broadcasted_iota(jnp.int32, (LANES, LANES), 1) // HEAD_DIM
    bd = (ri == ci).astype(F32)
    ss = jnp.dot(o * o, bd, preferred_element_type=F32, precision=lax.Precision.HIGHEST)
    return o * lax.rsqrt(ss * (1.0 / HEAD_DIM) + eps) * g


_ATT_TQ = 256
_ATT_TK = 512


def _transpose_v(v_ref, vt_scr):
    for c in range(v_ref.shape[1] // _ATT_TK):
        sl = slice(c * _ATT_TK, (c + 1) * _ATT_TK)
        vt_scr[:, sl] = jnp.transpose(v_ref[0, sl, :].astype(F32)).astype(BF16)


def _scores_phase(qm, k_ref, s_scr, m_scr, slot):
    tq = qm.shape[0]
    mx = jnp.full((8, tq), -jnp.inf, F32)
    for c in range(k_ref.shape[1] // _ATT_TK):
        s = lax.dot_general(k_ref[0, c * _ATT_TK:(c + 1) * _ATT_TK, :], qm, _NT, preferred_element_type=F32)
        s_scr[slot, c] = s
        mx = jnp.maximum(mx, jnp.max(s.reshape(_ATT_TK // 8, 8, tq), axis=0))
    m_scr[slot] = mx


def _softmax_phase(vt_scr, rows, s_scr, m_scr, slot):
    m = jnp.max(m_scr[slot], axis=0, keepdims=True)
    tq = m.shape[1]
    lp = jnp.zeros((8, tq), F32)
    acc = jnp.zeros((rows.stop - rows.start, tq), F32)
    for c in range(s_scr.shape[1]):
        p = jnp.exp2(s_scr[slot, c] - m)
        lp = lp + jnp.sum(p.reshape(_ATT_TK // 8, 8, tq), axis=0)
        acc = acc + jnp.dot(vt_scr[rows, c * _ATT_TK:(c + 1) * _ATT_TK], p.astype(BF16),
                            preferred_element_type=F32)
    return acc / jnp.sum(lp, axis=0, keepdims=True)


def _attn_jobs(jobs, q_next, first, k_ref, vt_scr, rows, s_scr, m_scr):
    assert len(jobs) % 2 == 0

    @pl.when(first)
    def _():
        _scores_phase(jobs[0], k_ref, s_scr, m_scr, 0)

    outs = []
    for n in range(len(jobs)):
        nxt = jobs[n + 1] if n + 1 < len(jobs) else q_next
        _scores_phase(nxt, k_ref, s_scr, m_scr, (n + 1) % 2)
        outs.append(_softmax_phase(vt_scr, rows[n], s_scr, m_scr, n % 2))
    return outs


_DIL_TQ = 128
_DIL_TW = 256


def _dil_kernel(q_ref, k_ref, v_ref, g_ref, o_ref, *scr):
    seq = q_ref.shape[1]
    lane = lax.broadcasted_iota(jnp.int32, (1, LANES), 1)
    head0 = lane < HEAD_DIM
    rel = (lax.broadcasted_iota(jnp.int32, (_DIL_TQ, _DIL_TW), 1)
           - lax.broadcasted_iota(jnp.int32, (_DIL_TQ, _DIL_TW), 0))

    for pi, r in enumerate(A_DILATIONS):
        length = seq // r
        nblk = length // _DIL_TQ
        o_scr, m_scr, l_scr = scr[3 * pi:3 * pi + 3]

        def body(t, carry, r=r, length=length, nblk=nblk, o_scr=o_scr, m_scr=m_scr, l_scr=l_scr):
            c = t // nblk
            i = t % nblk
            ws = jnp.clip(i * _DIL_TQ - A_HALF_WINDOW, 0, length - _DIL_TW)
            qs = c + r * (i * _DIL_TQ)
            ks = c + r * ws
            if r == 1:
                qsl = pl.ds(pl.multiple_of(qs, _DIL_TQ), _DIL_TQ)
                ksl = pl.ds(pl.multiple_of(ks, A_HALF_WINDOW), _DIL_TW)
            else:
                qsl = pl.ds(qs, _DIL_TQ, stride=r)
                ksl = pl.ds(ks, _DIL_TW, stride=r)
            qb = q_ref[0, qsl, :]
            kb = k_ref[0, ksl, :].astype(BF16)
            vb = v_ref[0, ksl, :].astype(BF16)
            delta = rel + (ws - i * _DIL_TQ)
            valid = jnp.abs(delta) <= A_HALF_WINDOW
            res = []
            for hd in range(2):
                qm = jnp.where(head0 if hd == 0 else jnp.logical_not(head0), qb, 0.0).astype(BF16)
                s = lax.dot_general(qm, kb, _NT, preferred_element_type=F32)
                s = jnp.where(valid, s, NEG_INF)
                m = jnp.max(s, axis=1, keepdims=True)
                p = jnp.exp(s - m)
                l = jnp.sum(p, axis=1, keepdims=True)
                o = jnp.dot(p.astype(BF16), vb, preferred_element_type=F32)
                res.append((o, m, l))
            o_scr[qsl, :] = jnp.where(head0, res[0][0], res[1][0])
            m_scr[qsl, :] = jnp.where(head0, res[0][1], res[1][1])
            l_scr[qsl, :] = jnp.where(head0, res[0][2], res[1][2])
            return carry

        lax.fori_loop(0, r * nblk, body, 0, unroll=8)

    rows = 256
    g = g_ref[...]

    def combine(t, carry):
        sl = pl.ds(pl.multiple_of(t * rows, rows), rows)
        ms = [scr[3 * pi + 1][sl, :] for pi in range(3)]
        mm = jnp.maximum(jnp.maximum(ms[0], ms[1]), ms[2])
        num = jnp.zeros((rows, LANES), F32)
        den = jnp.zeros((rows, LANES), F32)
        for pi in range(3):
            e = jnp.exp(ms[pi] - mm)
            num = num + e * scr[3 * pi][sl, :]
            den = den + e * scr[3 * pi + 2][sl, :]
        o_ref[0, sl, :] = _head_rmsnorm(num / den, g, NORM_EPS).astype(o_ref.dtype)
        return carry

    lax.fori_loop(0, seq // rows, combine, 0)


def _dil_call(qkv_a, g):
    bsz, seq, _ = qkv_a.shape
    n_slab = A_W // LANES
    blk = (1, seq, LANES)
    return pl.pallas_call(
        _dil_kernel,
        grid=(bsz, n_slab),
        in_specs=[pl.BlockSpec(blk, lambda b, j: (b, 0, j)),
                  pl.BlockSpec(blk, lambda b, j: (b, 0, n_slab + j)),
                  pl.BlockSpec(blk, lambda b, j: (b, 0, 2 * n_slab + j)),
                  pl.BlockSpec((1, LANES), lambda b, j: (0, j))],
        out_specs=pl.BlockSpec(blk, lambda b, j: (b, 0, j)),
        out_shape=jax.ShapeDtypeStruct((bsz, seq, A_W), BF16),
        scratch_shapes=[pltpu.VMEM((seq, LANES), F32) for _ in range(9)],
        compiler_params=_cparams(("parallel", "parallel")),
        name="dilated_attn",
    )(qkv_a, qkv_a, qkv_a, g.reshape(1, A_W))


def _attn_scratch(seq):
    return [pltpu.VMEM((LANES, seq), BF16),
            pltpu.VMEM((2, seq // _ATT_TK, _ATT_TK, _ATT_TQ), F32),
            pltpu.VMEM((2, 8, _ATT_TQ), F32)]


_DIFF_TILES = 2


def _diff_kernel(lam_ref, q_ref, qn_ref, k_ref, v_ref, g_ref, o_ref, vt_scr, s_scr, m_scr):
    first = pl.program_id(2) == 0
    lane = lax.broadcasted_iota(jnp.int32, (1, LANES), 1)
    lam = lam_ref[0]

    def qmap(q, mp):
        lo = mp * B_QK_DIM
        return jnp.where((lane >= lo) & (lane < lo + B_QK_DIM), q, jnp.zeros_like(q))

    @pl.when(first)
    def _():
        _transpose_v(v_ref, vt_scr)

    def run(n_heads):
        nm = 2 * n_heads
        jobs = [qmap(q_ref[0, t * _ATT_TQ:(t + 1) * _ATT_TQ, :], mp)
                for t in range(_DIFF_TILES) for mp in range(nm)]
        rows = [slice((mp // 2) * HEAD_DIM, (mp // 2 + 1) * HEAD_DIM) for mp in range(nm)] * _DIFF_TILES
        outs = _attn_jobs(jobs, qmap(qn_ref[0], 0), first, k_ref, vt_scr, rows, s_scr, m_scr)
        for t in range(_DIFF_TILES):
            heads = [_head_rmsnorm_t(outs[t * nm + 2 * hd] - lam * outs[t * nm + 2 * hd + 1],
                                     g_ref[hd * HEAD_DIM:(hd + 1) * HEAD_DIM, :], SUBLN_EPS)
                     for hd in range(n_heads)]
            if n_heads == 1:
                heads.append(jnp.zeros_like(heads[0]))
            o_ref[0, t * _ATT_TQ:(t + 1) * _ATT_TQ, :] = jnp.transpose(
                jnp.concatenate(heads, axis=0)).astype(o_ref.dtype)

    @pl.when(pl.program_id(1) < B_HEADS // 2)
    def _():
        run(2)

    @pl.when(pl.program_id(1) == B_HEADS // 2)
    def _():
        run(1)


def _diff_call(qkv_b, lam, g):
    bsz, seq, _ = qkv_b.shape
    nq = seq // (_DIFF_TILES * _ATT_TQ)
    kv = (1, seq, LANES)
    last = seq // _ATT_TQ - 1
    return pl.pallas_call(
        _diff_kernel,
        grid=(bsz, B_SLABS, nq),
        in_specs=[pl.BlockSpec(memory_space=pltpu.SMEM),
                  pl.BlockSpec((1, _DIFF_TILES * _ATT_TQ, LANES), lambda b, j, i: (b, i, j)),
                  pl.BlockSpec((1, _ATT_TQ, LANES),
                               lambda b, j, i: (b, jnp.minimum(_DIFF_TILES * (i + 1), last), j)),
                  pl.BlockSpec(kv, lambda b, j, i: (b, 0, B_SLABS + j)),
                  pl.BlockSpec(kv, lambda b, j, i: (b, 0, 2 * B_SLABS + j)),
                  pl.BlockSpec((LANES, _ATT_TQ), lambda b, j, i: (0, 0))],
        out_specs=pl.BlockSpec((1, _DIFF_TILES * _ATT_TQ, LANES), lambda b, j, i: (b, i, j)),
        out_shape=jax.ShapeDtypeStruct((bsz, seq, B_W), BF16),
        scratch_shapes=_attn_scratch(seq),
        compiler_params=_cparams(("parallel", "parallel", "arbitrary")),
        name="diff_attn",
    )(lam, qkv_b, qkv_b, qkv_b, qkv_b, g)


_MLA_TILES = 4


def _mla_kernel(q_ref, qn_ref, k_ref, v_ref, g_ref, o_ref, vt_scr, s_scr, m_scr):
    first = pl.program_id(2) == 0

    @pl.when(first)
    def _():
        _transpose_v(v_ref, vt_scr)

    jobs = [q_ref[0, n * _ATT_TQ:(n + 1) * _ATT_TQ, :] for n in range(_MLA_TILES)]
    rows = [slice(0, C_V_DIM)] * _MLA_TILES
    outs = _attn_jobs(jobs, qn_ref[0], first, k_ref, vt_scr, rows, s_scr, m_scr)
    for n, o in enumerate(outs):
        on = _head_rmsnorm_t(o, g_ref[:C_V_DIM, :], NORM_EPS)
        ot = jnp.concatenate([on, jnp.zeros_like(on)], axis=0)
        o_ref[0, n * _ATT_TQ:(n + 1) * _ATT_TQ, :] = jnp.transpose(ot).astype(o_ref.dtype)


def _mla_call(qc, kc, vc, g):
    bsz, seq, _ = qc.shape
    nq = seq // (_MLA_TILES * _ATT_TQ)
    kv = (1, seq, LANES)
    last = seq // _ATT_TQ - 1
    return pl.pallas_call(
        _mla_kernel,
        grid=(bsz, C_HEADS, nq),
        in_specs=[pl.BlockSpec((1, _MLA_TILES * _ATT_TQ, LANES), lambda b, j, i: (b, i, j)),
                  pl.BlockSpec((1, _ATT_TQ, LANES),
                               lambda b, j, i: (b, jnp.minimum(_MLA_TILES * (i + 1), last), j)),
                  pl.BlockSpec(kv, lambda b, j, i: (b, 0, j)),
                  pl.BlockSpec(kv, lambda b, j, i: (b, 0, j)),
                  pl.BlockSpec((LANES, _ATT_TQ), lambda b, j, i: (j, 0))],
        out_specs=pl.BlockSpec((1, _MLA_TILES * _ATT_TQ, LANES), lambda b, j, i: (b, i, j)),
        out_shape=jax.ShapeDtypeStruct((bsz, seq, C_W), BF16),
        scratch_shapes=_attn_scratch(seq),
        compiler_params=_cparams(("parallel", "parallel", "arbitrary")),
        name="latent_attn",
    )(qc, qc, kc, vc, g)


def _out_kernel(oa_ref, ob_ref, oc_ref, x_ref, wa_ref, wb_ref, wc_ref, g_ref, wr_ref,
                x1_ref, h_ref, lg_ref):
    y = jnp.dot(oa_ref[0], wa_ref[...], preferred_element_type=F32)
    y = y + jnp.dot(ob_ref[0], wb_ref[...], preferred_element_type=F32)
    y = y + jnp.dot(oc_ref[0], wc_ref[...], preferred_element_type=F32)
    x1 = x_ref[0] + y
    x1_ref[0] = x1
    ms = jnp.mean(x1 * x1, axis=-1, keepdims=True)
    h = x1 * lax.rsqrt(ms + NORM_EPS) * g_ref[...]
    h_hi = h.astype(BF16)
    h_ref[0] = h_hi
    h_lo = (h - h_hi.astype(F32)).astype(BF16)
    t = jnp.dot(h_hi, wr_ref[...], preferred_element_type=F32)
    lg_ref[0] = (t[:, :LANES] + t[:, LANES:]
                 + jnp.dot(h_lo, wr_ref[:, :LANES], preferred_element_type=F32))


def _out_call(oa, ob, oc, x, wo_a, wo_b, wo_c, g, w_router, tm=512):
    bsz, seq, d = x.shape
    row = lambda b, i: (b, i, 0)
    const = lambda b, i: (0, 0)
    wr = _pad_cols(w_router, LANES)
    wr_hi = wr.astype(BF16)
    wr = jnp.concatenate([wr_hi, (wr - wr_hi.astype(F32)).astype(BF16)], axis=1)
    return pl.pallas_call(
        _out_kernel,
        grid=(bsz, seq // tm),
        in_specs=[pl.BlockSpec((1, tm, A_W), row),
                  pl.BlockSpec((1, tm, B_W), row),
                  pl.BlockSpec((1, tm, C_W), row),
                  pl.BlockSpec((1, tm, d), row),
                  pl.BlockSpec((A_W, d), const),
                  pl.BlockSpec((B_W, d), const),
                  pl.BlockSpec((C_W, d), const),
                  pl.BlockSpec((1, d), const),
                  pl.BlockSpec((d, 2 * LANES), const)],
        out_specs=[pl.BlockSpec((1, tm, d), row),
                   pl.BlockSpec((1, tm, d), row),
                   pl.BlockSpec((1, tm, LANES), row)],
        out_shape=[jax.ShapeDtypeStruct((bsz, seq, d), F32),
                   jax.ShapeDtypeStruct((bsz, seq, d), BF16),
                   jax.ShapeDtypeStruct((bsz, seq, LANES), F32)],
        compiler_params=_cparams(("parallel", "parallel")),
        name="out_proj",
    )(oa, ob, oc, x, wo_a, wo_b, wo_c, g.reshape(1, d), wr)


_MOE_CHUNK = 256


def _router_kernel(lg_ref, pos_ref, post_ref, aff_ref, bnd_ref, pre_scr, *, cap):
    seq = lg_ref.shape[1]
    lt = jnp.transpose(lg_ref[0])[:N_EXPERTS]
    ex = jnp.exp(lt - jnp.max(lt, axis=0, keepdims=True))
    aff = ex / jnp.sum(ex, axis=0, keepdims=True)
    bits = pltpu.bitcast(aff, jnp.int32)

    def step(k, v):
        cand = v | lax.shift_left(jnp.int32(1), 30 - k)
        cnt = jnp.sum((bits >= cand).astype(jnp.int32), axis=1, keepdims=True)
        return jnp.where(cnt >= cap, cand, v)

    kth = lax.fori_loop(0, 31, step, jnp.zeros((N_EXPERTS, 1), jnp.int32))
    gt = bits > kth
    eq = bits == kth
    need = (cap - jnp.sum(gt.astype(jnp.int32), axis=1, keepdims=True)).astype(F32)

    flags = jnp.concatenate([gt, eq], axis=0)
    tri = (lax.broadcasted_iota(jnp.int32, (LANES, LANES), 0)
           <= lax.broadcasted_iota(jnp.int32, (LANES, LANES), 1)).astype(BF16)
    carry = jnp.zeros((2 * N_EXPERTS, 1), F32)
    bnd_ref[0] = jnp.zeros((N_EXPERTS, LANES), F32)
    for j in range(seq // LANES):
        if j % (_MOE_CHUNK // LANES) == 0:
            t = j // (_MOE_CHUNK // LANES)
            bnd_ref[0, :, t:t + 1] = carry[:N_EXPERTS] + jnp.minimum(carry[N_EXPERTS:], need)
        blk = jnp.where(flags[:, j * LANES:(j + 1) * LANES], 1.0, 0.0)
        inc = jnp.dot(blk.astype(BF16), tri, preferred_element_type=F32)
        pre_scr[:, j * LANES:(j + 1) * LANES] = inc - blk + carry
        carry = carry + inc[:, LANES - 1:LANES]
    n_chunks = seq // _MOE_CHUNK
    bnd_ref[0, :, n_chunks:n_chunks + 1] = carry[:N_EXPERTS] + jnp.minimum(carry[N_EXPERTS:], need)
    pre_gt = pre_scr[:N_EXPERTS]
    pre_eq = pre_scr[N_EXPERTS:]
    sel = jnp.logical_or(gt, jnp.logical_and(eq, pre_eq < need))
    pos = jnp.where(sel, pre_gt + jnp.minimum(pre_eq, need), -1.0)
    for c in range(n_chunks):
        sl = slice(c * _MOE_CHUNK, (c + 1) * _MOE_CHUNK)
        pos_ref[0, c] = pos[:, sl].astype(jnp.int32)
        aff_ref[0, c] = aff[:, sl]
    pad = jnp.full((LANES - N_EXPERTS, seq), -1.0, F32)
    post_ref[0] = jnp.transpose(jnp.concatenate([pos, pad], axis=0))


def _router_call(logits, cap):
    bsz, seq, _ = logits.shape
    chunked = (1, seq // _MOE_CHUNK, N_EXPERTS, _MOE_CHUNK)
    return pl.pallas_call(
        functools.partial(_router_kernel, cap=cap),
        grid=(bsz,),
        in_specs=[pl.BlockSpec((1, seq, LANES), lambda b: (b, 0, 0))],
        out_specs=[pl.BlockSpec(chunked, lambda b: (b, 0, 0, 0)),
                   pl.BlockSpec((1, seq, LANES), lambda b: (b, 0, 0)),
                   pl.BlockSpec(chunked, lambda b: (b, 0, 0, 0)),
                   pl.BlockSpec((1, N_EXPERTS, LANES), lambda b: (b, 0, 0))],
        out_shape=[jax.ShapeDtypeStruct((bsz,) + chunked[1:], jnp.int32),
                   jax.ShapeDtypeStruct((bsz, seq, LANES), F32),
                   jax.ShapeDtypeStruct((bsz,) + chunked[1:], F32),
                   jax.ShapeDtypeStruct((bsz, N_EXPERTS, LANES), F32)],
        scratch_shapes=[pltpu.VMEM((2 * N_EXPERTS, seq), F32)],
        compiler_params=_cparams(("parallel",)),
        name="router",
    )(logits)


def _gather_kernel(pos_ref, aff_ref, h_ref, xe_ref, gate_ref, p_scr, *, cap):
    e = pl.program_id(1)
    slot = lax.broadcasted_iota(jnp.int32, (cap, _MOE_CHUNK), 0)
    gate = jnp.zeros((cap, 1), F32)
    for c in range(pos_ref.shape[1]):
        hit = pos_ref[0, c, pl.ds(e, 1), :] == slot
        p_scr[:, c * _MOE_CHUNK:(c + 1) * _MOE_CHUNK] = jnp.where(hit, 1.0, 0.0).astype(BF16)
        gate = gate + jnp.sum(jnp.where(hit, aff_ref[0, c, pl.ds(e, 1), :], 0.0), axis=1, keepdims=True)
    xe_ref[0] = jnp.dot(p_scr[...], h_ref[0], preferred_element_type=F32).astype(BF16)
    gate_ref[0] = gate


def _gather_call(pos, aff, h_bf16, cap):
    bsz, seq, d = h_bf16.shape
    chunked = (1, seq // _MOE_CHUNK, N_EXPERTS, _MOE_CHUNK)
    return pl.pallas_call(
        functools.partial(_gather_kernel, cap=cap),
        grid=(bsz, N_EXPERTS),
        in_specs=[pl.BlockSpec(chunked, lambda b, e: (b, 0, 0, 0)),
                  pl.BlockSpec(chunked, lambda b, e: (b, 0, 0, 0)),
                  pl.BlockSpec((1, seq, d), lambda b, e: (b, 0, 0))],
        out_specs=[pl.BlockSpec((1, cap, d), lambda b, e: (e, b, 0)),
                   pl.BlockSpec((1, cap, 1), lambda b, e: (e, b, 0))],
        out_shape=[jax.ShapeDtypeStruct((N_EXPERTS, bsz * cap, d), BF16),
                   jax.ShapeDtypeStruct((N_EXPERTS, bsz * cap, 1), F32)],
        scratch_shapes=[pltpu.VMEM((cap, seq), BF16)],
        compiler_params=_cparams(("parallel", "arbitrary")),
        name="moe_gather",
    )(pos, aff, h_bf16)


def _expert_kernel(xe_ref, gate_ref, wg_ref, wu_ref, wd_ref, y_ref, wg_scr, wu_scr, wd_scr):
    @pl.when(pl.program_id(1) == 0)
    def _():
        wg_scr[...] = wg_ref[0].astype(BF16)
        wu_scr[...] = wu_ref[0].astype(BF16)
        wd_scr[...] = wd_ref[0].astype(BF16)

    xe = xe_ref[0]
    a = jnp.dot(xe, wg_scr[...], preferred_element_type=F32)
    u = jnp.dot(xe, wu_scr[...], preferred_element_type=F32)
    hmid = (a * jax.nn.sigmoid(a) * u).astype(BF16)
    y = jnp.dot(hmid, wd_scr[...], preferred_element_type=F32)
    y_ref[0] = (y * gate_ref[0]).astype(y_ref.dtype)


def _expert_call(xe, gate, w_gate, w_up, w_down, cap):
    n_e, m, d = xe.shape
    bsz = m // cap
    once = pl.Buffered(1)
    return pl.pallas_call(
        _expert_kernel,
        grid=(n_e, bsz),
        in_specs=[pl.BlockSpec((1, cap, d), lambda e, b: (e, b, 0)),
                  pl.BlockSpec((1, cap, 1), lambda e, b: (e, b, 0)),
                  pl.BlockSpec((1, d, EXPERT_FF), lambda e, b: (e, 0, 0), pipeline_mode=once),
                  pl.BlockSpec((1, d, EXPERT_FF), lambda e, b: (e, 0, 0), pipeline_mode=once),
                  pl.BlockSpec((1, EXPERT_FF, d), lambda e, b: (e, 0, 0), pipeline_mode=once)],
        out_specs=pl.BlockSpec((1, cap, d), lambda e, b: (b, e, 0)),
        out_shape=jax.ShapeDtypeStruct((bsz, n_e * cap, d), BF16),
        scratch_shapes=[pltpu.VMEM((d, EXPERT_FF), BF16), pltpu.VMEM((d, EXPERT_FF), BF16),
                        pltpu.VMEM((EXPERT_FF, d), BF16)],
        compiler_params=_cparams(("parallel", "arbitrary")),
        name="experts",
    )(xe, gate, w_gate, w_up, w_down)


_SCATTER_WIN = 256
_WIN_ALIGN = 16


def _scatter_kernel(win_ref, post_ref, x1_ref, y_ref, o_ref, *, cap):
    n_tiles = pl.num_programs(2)
    base = ((pl.program_id(0) * n_tiles + pl.program_id(2)) * N_EXPERTS) * 2
    slot = lax.broadcasted_iota(jnp.int32, (_MOE_CHUNK, _SCATTER_WIN), 1).astype(F32)

    def onehot(col):
        return jnp.where(col == slot, 1.0, 0.0).astype(BF16)

    acc = x1_ref[0]
    for e in range(N_EXPERTS):
        w0 = win_ref[base + 2 * e]
        col = post_ref[0, :, e:e + 1] - w0.astype(F32)
        rows = pl.ds(pl.multiple_of(e * cap + w0, _WIN_ALIGN), _SCATTER_WIN)
        acc = acc + jnp.dot(onehot(col), y_ref[0, rows, :], preferred_element_type=F32)
    o_ref[0] = acc

    tail = cap - _SCATTER_WIN
    for e in range(N_EXPERTS):
        @pl.when(win_ref[base + 2 * e + 1] != 0)
        def _(e=e):
            col = post_ref[0, :, e:e + 1]
            done = (win_ref[base + 2 * e] + _SCATTER_WIN).astype(F32)
            col = jnp.where(col >= done, col - tail, -1.0)
            o_ref[0] += jnp.dot(onehot(col), y_ref[0, e * cap + tail:(e + 1) * cap, :],
                                preferred_element_type=F32)


def _scatter_call(win, post, x1, y, cap, dsplit=2):
    bsz, seq, d = x1.shape
    dw = d // dsplit
    assert cap <= 2 * _SCATTER_WIN
    return pl.pallas_call(
        functools.partial(_scatter_kernel, cap=cap),
        grid_spec=pltpu.PrefetchScalarGridSpec(
            num_scalar_prefetch=1,
            grid=(bsz, dsplit, seq // _MOE_CHUNK),
            in_specs=[pl.BlockSpec((1, _MOE_CHUNK, LANES), lambda b, j, t, win: (b, t, 0)),
                      pl.BlockSpec((1, _MOE_CHUNK, dw), lambda b, j, t, win: (b, t, j)),
                      pl.BlockSpec((1, N_EXPERTS * cap, dw), lambda b, j, t, win: (b, 0, j))],
            out_specs=pl.BlockSpec((1, _MOE_CHUNK, dw), lambda b, j, t, win: (b, t, j))),
        out_shape=jax.ShapeDtypeStruct((bsz, seq, d), F32),
        compiler_params=_cparams(("parallel", "parallel", "arbitrary")),
        name="moe_scatter",
    )(win, post, x1, y)


def _norm_kernel(x_ref, g_ref, o_ref):
    x = x_ref[0]
    ms = jnp.mean(x * x, axis=-1, keepdims=True)
    o_ref[0] = x * lax.rsqrt(ms + NORM_EPS) * g_ref[...]


def _norm_call(x, g, tm=512):
    bsz, seq, d = x.shape
    row = lambda b, i: (b, i, 0)
    return pl.pallas_call(
        _norm_kernel,
        grid=(bsz, seq // tm),
        in_specs=[pl.BlockSpec((1, tm, d), row), pl.BlockSpec((1, d), lambda b, i: (0, 0))],
        out_specs=pl.BlockSpec((1, tm, d), row),
        out_shape=jax.ShapeDtypeStruct((bsz, seq, d), F32),
        compiler_params=_cparams(("parallel", "parallel")),
        name="final_norm",
    )(x, g.reshape(1, d))


def _expert_choice(h, logits, x1, w_gate, w_up, w_down):
    seq = h.shape[1]
    cap = CAPACITY_FACTOR * seq // N_EXPERTS
    pos, post, aff, bnd = _router_call(logits, cap)
    bnd = bnd.astype(jnp.int32)
    n_chunks = seq // _MOE_CHUNK
    start, stop = bnd[:, :, :n_chunks], bnd[:, :, 1:n_chunks + 1]
    w0 = jnp.minimum(start // _WIN_ALIGN * _WIN_ALIGN, cap - _SCATTER_WIN)
    win = jnp.stack([w0, (stop > w0 + _SCATTER_WIN).astype(jnp.int32)], axis=-1)
    win = jnp.swapaxes(win, 1, 2)
    xe, gate = _gather_call(pos, aff, h, cap)
    y = _expert_call(xe, gate, w_gate, w_up, w_down, cap)
    return _scatter_call(win.reshape(-1), post, x1, y, cap)


def kernel(x, positions, attn_norm_g, w_in, lam_q1, lam_k1, lam_q2, lam_k2, diff_subln_g, mla_q_norm_g, mla_w_uq, mla_kv_norm_g, mla_w_ukv, dil_out_g, mla_out_g, w_out, ffn_norm_g, w_router, w_gate, w_up, w_down, final_norm_g):
    depth = w_in.shape[0]
    tabs = _rope_tables(positions)
    for l in range(depth):
        lam_init = 0.8 - 0.6 * math.exp(-0.3 * l)
        w_proj, gq, wuq, gkv, wukv, wo_a, wo_b, wo_c = _prep_layer(
            w_in[l], mla_q_norm_g[l], mla_w_uq[l], mla_kv_norm_g[l], mla_w_ukv[l], w_out[l])
        qkv_a, qkv_b, qc, kc, vc = _proj_call(x, attn_norm_g[l], w_proj, tabs, gq, wuq, gkv, wukv)

        oa = _dil_call(qkv_a, dil_out_g[l])
        lam = (jnp.exp(jnp.sum(lam_q1[l] * lam_k1[l])) - jnp.exp(jnp.sum(lam_q2[l] * lam_k2[l])) + lam_init)
        g_b = jnp.tile(diff_subln_g[l] * (1.0 - lam_init), LANES // B_V_DIM)
        g_b = jnp.broadcast_to(g_b[:, None], (LANES, _ATT_TQ))
        ob = _diff_call(qkv_b, lam.reshape(1), g_b)
        g_c = jnp.pad(mla_out_g[l].reshape(C_HEADS, C_V_DIM), ((0, 0), (0, LANES - C_V_DIM))).reshape(C_W)
        g_c = jnp.broadcast_to(g_c[:, None], (C_W, _ATT_TQ))
        oc = _mla_call(qc, kc, vc, g_c)

        x1, h, logits = _out_call(oa, ob, oc, x, wo_a, wo_b, wo_c, ffn_norm_g[l], w_router[l])
        x = _expert_choice(h, logits, x1, w_gate[l], w_up[l], w_down[l])
    return _norm_call(x, final_norm_g)
```

```python
import functools
import math

import numpy as np

import jax
import jax.numpy as jnp
from jax import lax
from jax.experimental import pallas as pl
from jax.experimental.pallas import tpu as pltpu

F32 = jnp.float32
BF16 = jnp.bfloat16
LANES = 128

D_MODEL = 1024
HEAD_DIM = 64
A_HEADS = 6
A_DILATIONS = (1, 4, 16)
A_HALF_WINDOW = 64
B_HEADS = 5
B_QK_DIM = 32
B_V_DIM = 64
C_HEADS = 5
C_Q_RANK = 192
C_KV_RANK = 128
C_NOPE_DIM = 64
C_ROPE_DIM = 32
C_V_DIM = 64
C_ROPE_THETA = 10000.0
ROPE_THETA = 500000.0
ROPE_FRACTION = 4
N_EXPERTS = 16
CAPACITY_FACTOR = 2
EXPERT_FF = 1408
NORM_EPS = 1e-6
SUBLN_EPS = 1e-5
NEG_INF = -1e30
LOG2E = math.log2(math.e)

A_W = A_HEADS * HEAD_DIM
B_SLABS = 3
B_W = B_SLABS * LANES
C_W = C_HEADS * LANES
CQ_PAD = 256
PROJ_W = 3 * A_W + 3 * B_W + CQ_PAD + LANES + LANES
N_TABS = 9

VMEM_LIMIT = 56 * 1024 * 1024

_NT = (((1,), (1,)), ((), ()))


def _cparams(sem):
    return pltpu.CompilerParams(dimension_semantics=sem, vmem_limit_bytes=VMEM_LIMIT)


_ROPE_LAYOUTS = ((HEAD_DIM // ROPE_FRACTION, ROPE_THETA, HEAD_DIM, 0),
                 (B_QK_DIM // ROPE_FRACTION, ROPE_THETA, B_QK_DIM, 0),
                 (C_ROPE_DIM, C_ROPE_THETA, LANES, C_NOPE_DIM))
_ROPE_COLS = LANES // 2
_ROPE_ONE = sum(r for r, _, _, _ in _ROPE_LAYOUTS)


def _rope_compact(positions):
    pos = positions.astype(F32)
    cols = []
    for rot_dim, theta, _, _ in _ROPE_LAYOUTS:
        inv = 1.0 / (theta ** (jnp.arange(0, rot_dim, 2, dtype=F32) / rot_dim))
        ang = pos[:, :, None] * inv
        cols += [jnp.cos(ang), jnp.sin(ang)]
    bsz, seq = pos.shape
    cols.append(jnp.ones((bsz, seq, 1), F32))
    cols.append(jnp.zeros((bsz, seq, _ROPE_COLS - _ROPE_ONE - 1), F32))
    return jnp.concatenate(cols + cols, axis=-1)


def _rope_expansion():
    e = np.zeros((_ROPE_COLS, N_TABS * LANES), np.float32)
    col = 0
    for li, (rot_dim, _, period, offset) in enumerate(_ROPE_LAYOUTS):
        half = rot_dim // 2
        cos0, sin0 = col, col + half
        col += rot_dim
        for lane in range(LANES):
            d = lane % period - offset
            c_t, s1_t, s2_t = (3 * li) * LANES + lane, (3 * li + 1) * LANES + lane, (3 * li + 2) * LANES + lane
            if 0 <= d < half:
                e[cos0 + d, c_t] = 1.0
                e[sin0 + d, s1_t] = -1.0
            elif half <= d < 2 * half:
                e[cos0 + d - half, c_t] = 1.0
                e[sin0 + d - half, s2_t] = 1.0
            else:
                e[_ROPE_ONE, c_t] = 1.0
    return jnp.asarray(np.concatenate([e, e], axis=0), BF16)


def _pad_cols(w, width):
    return jnp.pad(w, ((0, 0), (0, width - w.shape[1])))


def _prep_layer(w_in, mla_q_norm_g, mla_w_uq, mla_kv_norm_g, mla_w_ukv, w_out):
    a = A_W
    bqk = B_HEADS * 2 * B_QK_DIM
    bv = B_HEADS * B_V_DIM
    o = 0
    aq = w_in[:, o:o + a] * (HEAD_DIM ** -0.5); o += a
    ak = w_in[:, o:o + a]; o += a
    av = w_in[:, o:o + a]; o += a
    bq = w_in[:, o:o + bqk] * (B_QK_DIM ** -0.5 * LOG2E); o += bqk
    bk = w_in[:, o:o + bqk]; o += bqk
    bvv = w_in[:, o:o + bv]; o += bv
    cq = w_in[:, o:o + C_Q_RANK]; o += C_Q_RANK
    ckv = w_in[:, o:o + C_KV_RANK]; o += C_KV_RANK
    ckr = w_in[:, o:o + C_ROPE_DIM]
    ckr_slab = jnp.pad(ckr, ((0, 0), (C_NOPE_DIM, LANES - C_NOPE_DIM - C_ROPE_DIM)))
    w_proj = jnp.concatenate([aq, ak, av, _pad_cols(bq, B_W), _pad_cols(bk, B_W), _pad_cols(bvv, B_W),
                              _pad_cols(cq, CQ_PAD), ckv, ckr_slab], axis=1).astype(BF16)

    qd = C_NOPE_DIM + C_ROPE_DIM
    wuq = mla_w_uq.reshape(C_Q_RANK, C_HEADS, qd) * (qd ** -0.5 * LOG2E)
    wuq = jnp.pad(wuq, ((0, CQ_PAD - C_Q_RANK), (0, 0), (0, LANES - qd))).reshape(CQ_PAD, C_W).astype(BF16)
    wukv = mla_w_ukv.reshape(C_KV_RANK, C_HEADS, C_NOPE_DIM + C_V_DIM)
    wk = jnp.pad(wukv[:, :, :C_NOPE_DIM], ((0, 0), (0, 0), (0, LANES - C_NOPE_DIM))).reshape(C_KV_RANK, C_W)
    wv = jnp.pad(wukv[:, :, C_NOPE_DIM:], ((0, 0), (0, 0), (0, LANES - C_V_DIM))).reshape(C_KV_RANK, C_W)
    wukv_p = jnp.concatenate([wk, wv], axis=1).astype(BF16)
    gq = jnp.pad(mla_q_norm_g, (0, CQ_PAD - C_Q_RANK)).reshape(1, CQ_PAD)
    gkv = mla_kv_norm_g.reshape(1, C_KV_RANK)

    wo_a = w_out[:a].astype(BF16)
    wo_b = jnp.pad(w_out[a:a + bv], ((0, B_W - bv), (0, 0))).astype(BF16)
    wo_c = w_out[a + bv:].reshape(C_HEADS, C_V_DIM, D_MODEL)
    wo_c = jnp.pad(wo_c, ((0, 0), (0, LANES - C_V_DIM), (0, 0))).reshape(C_W, D_MODEL).astype(BF16)
    return w_proj, gq, wuq, gkv, wukv_p, wo_a, wo_b, wo_c


def _proj_kernel(x_ref, g_ref, w_ref, cs_ref, exp_ref, gq_ref, wuq_ref, gkv_ref, wukv_ref,
                 qa_ref, qb_ref, qc_ref, kc_ref, vc_ref):
    x = x_ref[0]
    ms = jnp.mean(x * x, axis=-1, keepdims=True)
    h = (x * lax.rsqrt(ms + NORM_EPS) * g_ref[...]).astype(BF16)
    p = jnp.dot(h, w_ref[...], preferred_element_type=F32)

    cs = cs_ref[0]
    cs_hi = cs.astype(BF16)
    cs_lo = (cs - cs_hi.astype(F32)).astype(BF16)
    lane = lax.broadcasted_iota(jnp.int32, cs.shape, 1)
    tabs = jnp.dot(jnp.where(lane < _ROPE_COLS, cs_hi, cs_lo), exp_ref[...], preferred_element_type=F32)

    def slab(i):
        return p[:, i * LANES:(i + 1) * LANES]

    def rope(xs, layout, shift):
        c = tabs[:, (3 * layout) * LANES:(3 * layout + 1) * LANES]
        s1 = tabs[:, (3 * layout + 1) * LANES:(3 * layout + 2) * LANES]
        s2 = tabs[:, (3 * layout + 2) * LANES:(3 * layout + 3) * LANES]
        return xs * c + pltpu.roll(xs, LANES - shift, 1) * s1 + pltpu.roll(xs, shift, 1) * s2

    n_a = A_W // LANES
    for i in range(3 * n_a):
        v = slab(i)
        if i < 2 * n_a:
            v = rope(v, 0, HEAD_DIM // ROPE_FRACTION // 2)
        qa_ref[0, :, i * LANES:(i + 1) * LANES] = v
    base = 3 * n_a
    for i in range(3 * B_SLABS):
        v = slab(base + i)
        if i < 2 * B_SLABS:
            v = rope(v, 1, B_QK_DIM // ROPE_FRACTION // 2)
        qb_ref[0, :, i * LANES:(i + 1) * LANES] = v.astype(BF16)
    base += 3 * B_SLABS

    cq = p[:, base * LANES:base * LANES + CQ_PAD]
    msq = jnp.sum(cq * cq, axis=-1, keepdims=True) * (1.0 / C_Q_RANK)
    cqn = (cq * lax.rsqrt(msq + NORM_EPS) * gq_ref[...]).astype(BF16)
    qf = jnp.dot(cqn, wuq_ref[...], preferred_element_type=F32)
    base += CQ_PAD // LANES
    ckv = slab(base)
    mskv = jnp.mean(ckv * ckv, axis=-1, keepdims=True)
    ckvn = (ckv * lax.rsqrt(mskv + NORM_EPS) * gkv_ref[...]).astype(BF16)
    kvf = jnp.dot(ckvn, wukv_ref[...], preferred_element_type=F32)
    kr = rope(slab(base + 1), 2, C_ROPE_DIM // 2)
    for hd in range(C_HEADS):
        sl = slice(hd * LANES, (hd + 1) * LANES)
        qc_ref[0, :, sl] = rope(qf[:, sl], 2, C_ROPE_DIM // 2).astype(BF16)
        kc_ref[0, :, sl] = (kvf[:, sl] + kr).astype(BF16)
        vc_ref[0, :, sl] = kvf[:, C_W + hd * LANES:C_W + (hd + 1) * LANES].astype(BF16)


def _proj_call(x, g, w_proj, cs, expand, gq, wuq, gkv, wukv, tm=512):
    bsz, seq, d = x.shape
    grid = (bsz, seq // tm)
    row = lambda b, i: (b, i, 0)
    const = lambda b, i: (0, 0)
    out_w = (3 * A_W, 3 * B_W, C_W, C_W, C_W)
    out_dt = (F32, BF16, BF16, BF16, BF16)
    return pl.pallas_call(
        _proj_kernel,
        grid=grid,
        in_specs=[pl.BlockSpec((1, tm, d), row),
                  pl.BlockSpec((1, d), const),
                  pl.BlockSpec((d, PROJ_W), const),
                  pl.BlockSpec((1, tm, LANES), row),
                  pl.BlockSpec((LANES, N_TABS * LANES), const),
                  pl.BlockSpec((1, CQ_PAD), const),
                  pl.BlockSpec((CQ_PAD, C_W), const),
                  pl.BlockSpec((1, C_KV_RANK), const),
                  pl.BlockSpec((C_KV_RANK, 2 * C_W), const)],
        out_specs=[pl.BlockSpec((1, tm, w), row) for w in out_w],
        out_shape=[jax.ShapeDtypeStruct((bsz, seq, w), dt) for w, dt in zip(out_w, out_dt)],
        compiler_params=_cparams(("parallel", "parallel")),
        name="proj",
    )(x, g.reshape(1, d), w_proj, cs, expand, gq, wuq, gkv, wukv)


def _head_rmsnorm_t(ot, g_t, eps):
    ss = jnp.sum(ot * ot, axis=0, keepdims=True) * (1.0 / HEAD_DIM)
    return ot * lax.rsqrt(ss + eps) * g_t


def _head_rmsnorm(o, g, eps):
    ri = lax.broadcasted_iota(jnp.int32, (LANES, LANES), 0) // HEAD_DIM
    ci = lax.broadcasted_iota(jnp.int32, (LANES, LANES), 1) // HEAD_DIM
    bd = (ri == ci).astype(F32)
    ss = jnp.dot(o * o, bd, preferred_element_type=F32, precision=lax.Precision.HIGHEST)
    return o * lax.rsqrt(ss * (1.0 / HEAD_DIM) + eps) * g


_ATT_TQ = 256
_ATT_TK = 512


def _transpose_v(v_ref, vt_scr):
    for c in range(v_ref.shape[1] // _ATT_TK):
        sl = slice(c * _ATT_TK, (c + 1) * _ATT_TK)
        vt_scr[:, sl] = jnp.transpose(v_ref[0, sl, :].astype(F32)).astype(BF16)


def _scores_phase(qm, k_ref, s_scr, m_scr, slot):
    tq = qm.shape[0]
    mx = jnp.full((8, tq), -jnp.inf, F32)
    for c in range(k_ref.shape[1] // _ATT_TK):
        s = lax.dot_general(k_ref[0, c * _ATT_TK:(c + 1) * _ATT_TK, :], qm, _NT, preferred_element_type=F32)
        s_scr[slot, c] = s
        mx = jnp.maximum(mx, jnp.max(s.reshape(_ATT_TK // 8, 8, tq), axis=0))
    m_scr[slot] = mx


def _softmax_phase(vt_scr, rows, s_scr, m_scr, slot):
    m = jnp.max(m_scr[slot], axis=0, keepdims=True)
    tq = m.shape[1]
    lp = jnp.zeros((8, tq), F32)
    acc = jnp.zeros((rows.stop - rows.start, tq), F32)
    for c in range(s_scr.shape[1]):
        p = jnp.exp2(s_scr[slot, c] - m)
        lp = lp + jnp.sum(p.reshape(_ATT_TK // 8, 8, tq), axis=0)
        acc = acc + jnp.dot(vt_scr[rows, c * _ATT_TK:(c + 1) * _ATT_TK], p.astype(BF16),
                            preferred_element_type=F32)
    return acc / jnp.sum(lp, axis=0, keepdims=True)


def _attn_jobs(jobs, q_next, first, k_ref, vt_scr, rows, s_scr, m_scr):
    assert len(jobs) % 2 == 0

    @pl.when(first)
    def _():
        _scores_phase(jobs[0], k_ref, s_scr, m_scr, 0)

    outs = []
    for n in range(len(jobs)):
        nxt = jobs[n + 1] if n + 1 < len(jobs) else q_next
        _scores_phase(nxt, k_ref, s_scr, m_scr, (n + 1) % 2)
        outs.append(_softmax_phase(vt_scr, rows[n], s_scr, m_scr, n % 2))
    return outs


_DIL_TQ = 128
_DIL_TW = 256


def _dil_kernel(q_ref, k_ref, v_ref, g_ref, o_ref, *scr):
    seq = q_ref.shape[1]
    lane = lax.broadcasted_iota(jnp.int32, (1, LANES), 1)
    head0 = lane < HEAD_DIM
    rel = (lax.broadcasted_iota(jnp.int32, (_DIL_TQ, _DIL_TW), 1)
           - lax.broadcasted_iota(jnp.int32, (_DIL_TQ, _DIL_TW), 0))

    for pi, r in enumerate(A_DILATIONS):
        length = seq // r
        nblk = length // _DIL_TQ
        o_scr, m_scr, l_scr = scr[3 * pi:3 * pi + 3]

        def body(t, carry, r=r, length=length, nblk=nblk, o_scr=o_scr, m_scr=m_scr, l_scr=l_scr):
            c = t // nblk
            i = t % nblk
            ws = jnp.clip(i * _DIL_TQ - A_HALF_WINDOW, 0, length - _DIL_TW)
            qs = c + r * (i * _DIL_TQ)
            ks = c + r * ws
            if r == 1:
                qsl = pl.ds(pl.multiple_of(qs, _DIL_TQ), _DIL_TQ)
                ksl = pl.ds(pl.multiple_of(ks, A_HALF_WINDOW), _DIL_TW)
            else:
                qsl = pl.ds(qs, _DIL_TQ, stride=r)
                ksl = pl.ds(ks, _DIL_TW, stride=r)
            qb = q_ref[0, qsl, :]
            kb = k_ref[0, ksl, :].astype(BF16)
            vb = v_ref[0, ksl, :].astype(BF16)
            delta = rel + (ws - i * _DIL_TQ)
            valid = jnp.abs(delta) <= A_HALF_WINDOW
            res = []
            for hd in range(2):
                qm = jnp.where(head0 if hd == 0 else jnp.logical_not(head0), qb, 0.0).astype(BF16)
                s = lax.dot_general(qm, kb, _NT, preferred_element_type=F32)
                s = jnp.where(valid, s, NEG_INF)
                m = jnp.max(s, axis=1, keepdims=True)
                p = jnp.exp(s - m)
                l = jnp.sum(p, axis=1, keepdims=True)
                o = jnp.dot(p.astype(BF16), vb, preferred_element_type=F32)
                res.append((o, m, l))
            o_scr[qsl, :] = jnp.where(head0, res[0][0], res[1][0])
            m_scr[qsl, :] = jnp.where(head0, res[0][1], res[1][1])
            l_scr[qsl, :] = jnp.where(head0, res[0][2], res[1][2])
            return carry

        lax.fori_loop(0, r * nblk, body, 0, unroll=8)

    rows = 256
    g = g_ref[...]

    def combine(t, carry):
        sl = pl.ds(pl.multiple_of(t * rows, rows), rows)
        ms = [scr[3 * pi + 1][sl, :] for pi in range(3)]
        mm = jnp.maximum(jnp.maximum(ms[0], ms[1]), ms[2])
        num = jnp.zeros((rows, LANES), F32)
        den = jnp.zeros((rows, LANES), F32)
        for pi in range(3):
            e = jnp.exp(ms[pi] - mm)
            num = num + e * scr[3 * pi][sl, :]
            den = den + e * scr[3 * pi + 2][sl, :]
        o_ref[0, sl, :] = _head_rmsnorm(num / den, g, NORM_EPS).astype(o_ref.dtype)
        return carry

    lax.fori_loop(0, seq // rows, combine, 0)


def _dil_call(qkv_a, g):
    bsz, seq, _ = qkv_a.shape
    n_slab = A_W // LANES
    blk = (1, seq, LANES)
    return pl.pallas_call(
        _dil_kernel,
        grid=(bsz, n_slab),
        in_specs=[pl.BlockSpec(blk, lambda b, j: (b, 0, j)),
                  pl.BlockSpec(blk, lambda b, j: (b, 0, n_slab + j)),
                  pl.BlockSpec(blk, lambda b, j: (b, 0, 2 * n_slab + j)),
                  pl.BlockSpec((1, LANES), lambda b, j: (0, j))],
        out_specs=pl.BlockSpec(blk, lambda b, j: (b, 0, j)),
        out_shape=jax.ShapeDtypeStruct((bsz, seq, A_W), BF16),
        scratch_shapes=[pltpu.VMEM((seq, LANES), F32) for _ in range(9)],
        compiler_params=_cparams(("parallel", "parallel")),
        name="dilated_attn",
    )(qkv_a, qkv_a, qkv_a, g.reshape(1, A_W))


def _attn_scratch(seq):
    return [pltpu.VMEM((LANES, seq), BF16),
            pltpu.VMEM((2, seq // _ATT_TK, _ATT_TK, _ATT_TQ), F32),
            pltpu.VMEM((2, 8, _ATT_TQ), F32)]


_DIFF_TILES = 2


def _diff_kernel(lam_ref, q_ref, qn_ref, k_ref, v_ref, g_ref, o_ref, vt_scr, s_scr, m_scr):
    first = pl.program_id(2) == 0
    lane = lax.broadcasted_iota(jnp.int32, (1, LANES), 1)
    lam = lam_ref[0]

    def qmap(q, mp):
        lo = mp * B_QK_DIM
        return jnp.where((lane >= lo) & (lane < lo + B_QK_DIM), q, jnp.zeros_like(q))

    @pl.when(first)
    def _():
        _transpose_v(v_ref, vt_scr)

    def run(n_heads):
        nm = 2 * n_heads
        jobs = [qmap(q_ref[0, t * _ATT_TQ:(t + 1) * _ATT_TQ, :], mp)
                for t in range(_DIFF_TILES) for mp in range(nm)]
        rows = [slice((mp // 2) * HEAD_DIM, (mp // 2 + 1) * HEAD_DIM) for mp in range(nm)] * _DIFF_TILES
        outs = _attn_jobs(jobs, qmap(qn_ref[0], 0), first, k_ref, vt_scr, rows, s_scr, m_scr)
        for t in range(_DIFF_TILES):
            heads = [_head_rmsnorm_t(outs[t * nm + 2 * hd] - lam * outs[t * nm + 2 * hd + 1],
                                     g_ref[hd * HEAD_DIM:(hd + 1) * HEAD_DIM, :], SUBLN_EPS)
                     for hd in range(n_heads)]
            if n_heads == 1:
                heads.append(jnp.zeros_like(heads[0]))
            o_ref[0, t * _ATT_TQ:(t + 1) * _ATT_TQ, :] = jnp.transpose(
                jnp.concatenate(heads, axis=0)).astype(o_ref.dtype)

    @pl.when(pl.program_id(1) < B_HEADS // 2)
    def _():
        run(2)

    @pl.when(pl.program_id(1) == B_HEADS // 2)
    def _():
        run(1)


def _diff_call(qkv_b, lam, g):
    bsz, seq, _ = qkv_b.shape
    nq = seq // (_DIFF_TILES * _ATT_TQ)
    kv = (1, seq, LANES)
    last = seq // _ATT_TQ - 1
    return pl.pallas_call(
        _diff_kernel,
        grid=(bsz, B_SLABS, nq),
        in_specs=[pl.BlockSpec(memory_space=pltpu.SMEM),
                  pl.BlockSpec((1, _DIFF_TILES * _ATT_TQ, LANES), lambda b, j, i: (b, i, j)),
                  pl.BlockSpec((1, _ATT_TQ, LANES),
                               lambda b, j, i: (b, jnp.minimum(_DIFF_TILES * (i + 1), last), j)),
                  pl.BlockSpec(kv, lambda b, j, i: (b, 0, B_SLABS + j)),
                  pl.BlockSpec(kv, lambda b, j, i: (b, 0, 2 * B_SLABS + j)),
                  pl.BlockSpec((LANES, _ATT_TQ), lambda b, j, i: (0, 0))],
        out_specs=pl.BlockSpec((1, _DIFF_TILES * _ATT_TQ, LANES), lambda b, j, i: (b, i, j)),
        out_shape=jax.ShapeDtypeStruct((bsz, seq, B_W), BF16),
        scratch_shapes=_attn_scratch(seq),
        compiler_params=_cparams(("parallel", "parallel", "arbitrary")),
        name="diff_attn",
    )(lam, qkv_b, qkv_b, qkv_b, qkv_b, g)


_MLA_TILES = 4


def _mla_kernel(q_ref, qn_ref, k_ref, v_ref, g_ref, o_ref, vt_scr, s_scr, m_scr):
    first = pl.program_id(2) == 0

    @pl.when(first)
    def _():
        _transpose_v(v_ref, vt_scr)

    jobs = [q_ref[0, n * _ATT_TQ:(n + 1) * _ATT_TQ, :] for n in range(_MLA_TILES)]
    rows = [slice(0, C_V_DIM)] * _MLA_TILES
    outs = _attn_jobs(jobs, qn_ref[0], first, k_ref, vt_scr, rows, s_scr, m_scr)
    for n, o in enumerate(outs):
        on = _head_rmsnorm_t(o, g_ref[:C_V_DIM, :], NORM_EPS)
        ot = jnp.concatenate([on, jnp.zeros_like(on)], axis=0)
        o_ref[0, n * _ATT_TQ:(n + 1) * _ATT_TQ, :] = jnp.transpose(ot).astype(o_ref.dtype)


def _mla_call(qc, kc, vc, g):
    bsz, seq, _ = qc.shape
    nq = seq // (_MLA_TILES * _ATT_TQ)
    kv = (1, seq, LANES)
    last = seq // _ATT_TQ - 1
    return pl.pallas_call(
        _mla_kernel,
        grid=(bsz, C_HEADS, nq),
        in_specs=[pl.BlockSpec((1, _MLA_TILES * _ATT_TQ, LANES), lambda b, j, i: (b, i, j)),
                  pl.BlockSpec((1, _ATT_TQ, LANES),
                               lambda b, j, i: (b, jnp.minimum(_MLA_TILES * (i + 1), last), j)),
                  pl.BlockSpec(kv, lambda b, j, i: (b, 0, j)),
                  pl.BlockSpec(kv, lambda b, j, i: (b, 0, j)),
                  pl.BlockSpec((LANES, _ATT_TQ), lambda b, j, i: (j, 0))],
        out_specs=pl.BlockSpec((1, _MLA_TILES * _ATT_TQ, LANES), lambda b, j, i: (b, i, j)),
        out_shape=jax.ShapeDtypeStruct((bsz, seq, C_W), BF16),
        scratch_shapes=_attn_scratch(seq),
        compiler_params=_cparams(("parallel", "parallel", "arbitrary")),
        name="latent_attn",
    )(qc, qc, kc, vc, g)


def _out_kernel(oa_ref, ob_ref, oc_ref, x_ref, wa_ref, wb_ref, wc_ref, g_ref, wr_ref,
                x1_ref, h_ref, lg_ref):
    y = jnp.dot(oa_ref[0], wa_ref[...], preferred_element_type=F32)
    y = y + jnp.dot(ob_ref[0], wb_ref[...], preferred_element_type=F32)
    y = y + jnp.dot(oc_ref[0], wc_ref[...], preferred_element_type=F32)
    x1 = x_ref[0] + y
    x1_ref[0] = x1
    ms = jnp.mean(x1 * x1, axis=-1, keepdims=True)
    h = x1 * lax.rsqrt(ms + NORM_EPS) * g_ref[...]
    h_hi = h.astype(BF16)
    h_ref[0] = h_hi
    h_lo = (h - h_hi.astype(F32)).astype(BF16)
    t = jnp.dot(h_hi, wr_ref[...], preferred_element_type=F32)
    lg_ref[0] = (t[:, :LANES] + t[:, LANES:]
                 + jnp.dot(h_lo, wr_ref[:, :LANES], preferred_element_type=F32))


def _out_call(oa, ob, oc, x, wo_a, wo_b, wo_c, g, w_router, tm=512):
    bsz, seq, d = x.shape
    row = lambda b, i: (b, i, 0)
    const = lambda b, i: (0, 0)
    wr = _pad_cols(w_router, LANES)
    wr_hi = wr.astype(BF16)
    wr = jnp.concatenate([wr_hi, (wr - wr_hi.astype(F32)).astype(BF16)], axis=1)
    return pl.pallas_call(
        _out_kernel,
        grid=(bsz, seq // tm),
        in_specs=[pl.BlockSpec((1, tm, A_W), row),
                  pl.BlockSpec((1, tm, B_W), row),
                  pl.BlockSpec((1, tm, C_W), row),
                  pl.BlockSpec((1, tm, d), row),
                  pl.BlockSpec((A_W, d), const),
                  pl.BlockSpec((B_W, d), const),
                  pl.BlockSpec((C_W, d), const),
                  pl.BlockSpec((1, d), const),
                  pl.BlockSpec((d, 2 * LANES), const)],
        out_specs=[pl.BlockSpec((1, tm, d), row),
                   pl.BlockSpec((1, tm, d), row),
                   pl.BlockSpec((1, tm, LANES), row)],
        out_shape=[jax.ShapeDtypeStruct((bsz, seq, d), F32),
                   jax.ShapeDtypeStruct((bsz, seq, d), BF16),
                   jax.ShapeDtypeStruct((bsz, seq, LANES), F32)],
        compiler_params=_cparams(("parallel", "parallel")),
        name="out_proj",
    )(oa, ob, oc, x, wo_a, wo_b, wo_c, g.reshape(1, d), wr)


_MOE_CHUNK = 256


def _router_kernel(lg_ref, pos_ref, post_ref, aff_ref, bnd_ref, pre_scr, *, cap):
    seq = lg_ref.shape[1]
    lt = jnp.transpose(lg_ref[0])[:N_EXPERTS]
    ex = jnp.exp(lt - jnp.max(lt, axis=0, keepdims=True))
    aff = ex / jnp.sum(ex, axis=0, keepdims=True)
    bits = pltpu.bitcast(aff, jnp.int32)

    def step(k, v):
        cand = v | lax.shift_left(jnp.int32(1), 30 - k)
        cnt = jnp.sum((bits >= cand).astype(jnp.int32), axis=1, keepdims=True)
        return jnp.where(cnt >= cap, cand, v)

    kth = lax.fori_loop(0, 31, step, jnp.zeros((N_EXPERTS, 1), jnp.int32))
    gt = bits > kth
    eq = bits == kth
    need = (cap - jnp.sum(gt.astype(jnp.int32), axis=1, keepdims=True)).astype(F32)

    flags = jnp.concatenate([gt, eq], axis=0)
    tri = (lax.broadcasted_iota(jnp.int32, (LANES, LANES), 0)
           <= lax.broadcasted_iota(jnp.int32, (LANES, LANES), 1)).astype(BF16)
    carry = jnp.zeros((2 * N_EXPERTS, 1), F32)
    bnd_ref[0] = jnp.zeros((N_EXPERTS, LANES), F32)
    for j in range(seq // LANES):
        if j % (_MOE_CHUNK // LANES) == 0:
            t = j // (_MOE_CHUNK // LANES)
            bnd_ref[0, :, t:t + 1] = carry[:N_EXPERTS] + jnp.minimum(carry[N_EXPERTS:], need)
        blk = jnp.where(flags[:, j * LANES:(j + 1) * LANES], 1.0, 0.0)
        inc = jnp.dot(blk.astype(BF16), tri, preferred_element_type=F32)
        pre_scr[:, j * LANES:(j + 1) * LANES] = inc - blk + carry
        carry = carry + inc[:, LANES - 1:LANES]
    n_chunks = seq // _MOE_CHUNK
    bnd_ref[0, :, n_chunks:n_chunks + 1] = carry[:N_EXPERTS] + jnp.minimum(carry[N_EXPERTS:], need)
    pre_gt = pre_scr[:N_EXPERTS]
    pre_eq = pre_scr[N_EXPERTS:]
    sel = jnp.logical_or(gt, jnp.logical_and(eq, pre_eq < need))
    pos = jnp.where(sel, pre_gt + jnp.minimum(pre_eq, need), -1.0)
    for c in range(n_chunks):
        sl = slice(c * _MOE_CHUNK, (c + 1) * _MOE_CHUNK)
        pos_ref[0, c] = pos[:, sl].astype(jnp.int32)
        aff_ref[0, c] = aff[:, sl]
    pad = jnp.full((LANES - N_EXPERTS, seq), -1.0, F32)
    post_ref[0] = jnp.transpose(jnp.concatenate([pos, pad], axis=0))


def _router_call(logits, cap):
    bsz, seq, _ = logits.shape
    chunked = (1, seq // _MOE_CHUNK, N_EXPERTS, _MOE_CHUNK)
    return pl.pallas_call(
        functools.partial(_router_kernel, cap=cap),
        grid=(bsz,),
        in_specs=[pl.BlockSpec((1, seq, LANES), lambda b: (b, 0, 0))],
        out_specs=[pl.BlockSpec(chunked, lambda b: (b, 0, 0, 0)),
                   pl.BlockSpec((1, seq, LANES), lambda b: (b, 0, 0)),
                   pl.BlockSpec(chunked, lambda b: (b, 0, 0, 0)),
                   pl.BlockSpec((1, N_EXPERTS, LANES), lambda b: (b, 0, 0))],
        out_shape=[jax.ShapeDtypeStruct((bsz,) + chunked[1:], jnp.int32),
                   jax.ShapeDtypeStruct((bsz, seq, LANES), F32),
                   jax.ShapeDtypeStruct((bsz,) + chunked[1:], F32),
                   jax.ShapeDtypeStruct((bsz, N_EXPERTS, LANES), F32)],
        scratch_shapes=[pltpu.VMEM((2 * N_EXPERTS, seq), F32)],
        compiler_params=_cparams(("parallel",)),
        name="router",
    )(logits)


def _gather_kernel(pos_ref, aff_ref, h_ref, xe_ref, gate_ref, p_scr, *, cap):
    e = pl.program_id(1)
    slot = lax.broadcasted_iota(jnp.int32, (cap, _MOE_CHUNK), 0)
    gate = jnp.zeros((cap, 1), F32)
    for c in range(pos_ref.shape[1]):
        hit = pos_ref[0, c, pl.ds(e, 1), :] == slot
        p_scr[:, c * _MOE_CHUNK:(c + 1) * _MOE_CHUNK] = jnp.where(hit, 1.0, 0.0).astype(BF16)
        gate = gate + jnp.sum(jnp.where(hit, aff_ref[0, c, pl.ds(e, 1), :], 0.0), axis=1, keepdims=True)
    xe_ref[0] = jnp.dot(p_scr[...], h_ref[0], preferred_element_type=F32).astype(BF16)
    gate_ref[0] = gate


def _gather_call(pos, aff, h_bf16, cap):
    bsz, seq, d = h_bf16.shape
    chunked = (1, seq // _MOE_CHUNK, N_EXPERTS, _MOE_CHUNK)
    return pl.pallas_call(
        functools.partial(_gather_kernel, cap=cap),
        grid=(bsz, N_EXPERTS),
        in_specs=[pl.BlockSpec(chunked, lambda b, e: (b, 0, 0, 0)),
                  pl.BlockSpec(chunked, lambda b, e: (b, 0, 0, 0)),
                  pl.BlockSpec((1, seq, d), lambda b, e: (b, 0, 0))],
        out_specs=[pl.BlockSpec((1, cap, d), lambda b, e: (e, b, 0)),
                   pl.BlockSpec((1, cap, 1), lambda b, e: (e, b, 0))],
        out_shape=[jax.ShapeDtypeStruct((N_EXPERTS, bsz * cap, d), BF16),
                   jax.ShapeDtypeStruct((N_EXPERTS, bsz * cap, 1), F32)],
        scratch_shapes=[pltpu.VMEM((cap, seq), BF16)],
        compiler_params=_cparams(("parallel", "arbitrary")),
        name="moe_gather",
    )(pos, aff, h_bf16)


def _expert_kernel(xe_ref, gate_ref, wg_ref, wu_ref, wd_ref, y_ref, wg_scr, wu_scr, wd_scr):
    @pl.when(pl.program_id(1) == 0)
    def _():
        wg_scr[...] = wg_ref[0].astype(BF16)
        wu_scr[...] = wu_ref[0].astype(BF16)
        wd_scr[...] = wd_ref[0].astype(BF16)

    xe = xe_ref[0]
    a = jnp.dot(xe, wg_scr[...], preferred_element_type=F32)
    u = jnp.dot(xe, wu_scr[...], preferred_element_type=F32)
    hmid = (a * jax.nn.sigmoid(a) * u).astype(BF16)
    y = jnp.dot(hmid, wd_scr[...], preferred_element_type=F32)
    y_ref[0] = (y * gate_ref[0]).astype(y_ref.dtype)


def _expert_call(xe, gate, w_gate, w_up, w_down, cap):
    n_e, m, d = xe.shape
    bsz = m // cap
    return pl.pallas_call(
        _expert_kernel,
        grid=(n_e, bsz),
        in_specs=[pl.BlockSpec((1, cap, d), lambda e, b: (e, b, 0)),
                  pl.BlockSpec((1, cap, 1), lambda e, b: (e, b, 0)),
                  pl.BlockSpec((1, d, EXPERT_FF), lambda e, b: (e, 0, 0)),
                  pl.BlockSpec((1, d, EXPERT_FF), lambda e, b: (e, 0, 0)),
                  pl.BlockSpec((1, EXPERT_FF, d), lambda e, b: (e, 0, 0))],
        out_specs=pl.BlockSpec((1, cap, d), lambda e, b: (b, e, 0)),
        out_shape=jax.ShapeDtypeStruct((bsz, n_e * cap, d), BF16),
        scratch_shapes=[pltpu.VMEM((d, EXPERT_FF), BF16), pltpu.VMEM((d, EXPERT_FF), BF16),
                        pltpu.VMEM((EXPERT_FF, d), BF16)],
        compiler_params=_cparams(("parallel", "arbitrary")),
        name="experts",
    )(xe, gate, w_gate, w_up, w_down)


_SCATTER_WIN = 256
_WIN_ALIGN = 16


def _scatter_kernel(win_ref, post_ref, x1_ref, y_ref, g_ref, o_ref, *, cap, final_norm):
    base = ((pl.program_id(0) * pl.num_programs(1) + pl.program_id(1)) * N_EXPERTS) * 2
    slot = lax.broadcasted_iota(jnp.int32, (_MOE_CHUNK, _SCATTER_WIN), 1).astype(F32)

    def onehot(col):
        return jnp.where(col == slot, 1.0, 0.0).astype(BF16)

    acc = x1_ref[0]
    for e in range(N_EXPERTS):
        w0 = win_ref[base + 2 * e]
        col = post_ref[0, :, e:e + 1] - w0.astype(F32)
        rows = pl.ds(pl.multiple_of(e * cap + w0, _WIN_ALIGN), _SCATTER_WIN)
        acc = acc + jnp.dot(onehot(col), y_ref[0, rows, :], preferred_element_type=F32)
    o_ref[0] = acc

    tail = cap - _SCATTER_WIN
    for e in range(N_EXPERTS):
        @pl.when(win_ref[base + 2 * e + 1] != 0)
        def _(e=e):
            col = post_ref[0, :, e:e + 1]
            done = (win_ref[base + 2 * e] + _SCATTER_WIN).astype(F32)
            col = jnp.where(col >= done, col - tail, -1.0)
            o_ref[0] += jnp.dot(onehot(col), y_ref[0, e * cap + tail:(e + 1) * cap, :],
                                preferred_element_type=F32)

    if final_norm:
        x = o_ref[0]
        ms = jnp.mean(x * x, axis=-1, keepdims=True)
        o_ref[0] = x * lax.rsqrt(ms + NORM_EPS) * g_ref[...]


def _scatter_call(win, post, x1, y, g, cap, final_norm):
    bsz, seq, d = x1.shape
    assert cap <= 2 * _SCATTER_WIN
    return pl.pallas_call(
        functools.partial(_scatter_kernel, cap=cap, final_norm=final_norm),
        grid_spec=pltpu.PrefetchScalarGridSpec(
            num_scalar_prefetch=1,
            grid=(bsz, seq // _MOE_CHUNK),
            in_specs=[pl.BlockSpec((1, _MOE_CHUNK, LANES), lambda b, t, win: (b, t, 0)),
                      pl.BlockSpec((1, _MOE_CHUNK, d), lambda b, t, win: (b, t, 0)),
                      pl.BlockSpec((1, N_EXPERTS * cap, d), lambda b, t, win: (b, 0, 0)),
                      pl.BlockSpec((1, d), lambda b, t, win: (0, 0))],
            out_specs=pl.BlockSpec((1, _MOE_CHUNK, d), lambda b, t, win: (b, t, 0))),
        out_shape=jax.ShapeDtypeStruct((bsz, seq, d), F32),
        compiler_params=_cparams(("parallel", "arbitrary")),
        name="moe_scatter",
    )(win, post, x1, y, g.reshape(1, d))


def _expert_choice(h, logits, x1, w_gate, w_up, w_down, final_g, final_norm):
    seq = h.shape[1]
    cap = CAPACITY_FACTOR * seq // N_EXPERTS
    pos, post, aff, bnd = _router_call(logits, cap)
    bnd = bnd.astype(jnp.int32)
    n_chunks = seq // _MOE_CHUNK
    start, stop = bnd[:, :, :n_chunks], bnd[:, :, 1:n_chunks + 1]
    w0 = jnp.minimum(start // _WIN_ALIGN * _WIN_ALIGN, cap - _SCATTER_WIN)
    win = jnp.stack([w0, (stop > w0 + _SCATTER_WIN).astype(jnp.int32)], axis=-1)
    win = jnp.swapaxes(win, 1, 2)
    xe, gate = _gather_call(pos, aff, h, cap)
    y = _expert_call(xe, gate, w_gate, w_up, w_down, cap)
    return _scatter_call(win.reshape(-1), post, x1, y, final_g, cap, final_norm)


def kernel(x, positions, attn_norm_g, w_in, lam_q1, lam_k1, lam_q2, lam_k2, diff_subln_g, mla_q_norm_g, mla_w_uq, mla_kv_norm_g, mla_w_ukv, dil_out_g, mla_out_g, w_out, ffn_norm_g, w_router, w_gate, w_up, w_down, final_norm_g):
    depth = w_in.shape[0]
    cs = _rope_compact(positions)
    expand = _rope_expansion()
    for l in range(depth):
        lam_init = 0.8 - 0.6 * math.exp(-0.3 * l)
        w_proj, gq, wuq, gkv, wukv, wo_a, wo_b, wo_c = _prep_layer(
            w_in[l], mla_q_norm_g[l], mla_w_uq[l], mla_kv_norm_g[l], mla_w_ukv[l], w_out[l])
        qkv_a, qkv_b, qc, kc, vc = _proj_call(x, attn_norm_g[l], w_proj, cs, expand, gq, wuq, gkv, wukv)

        oa = _dil_call(qkv_a, dil_out_g[l])
        lam = (jnp.exp(jnp.sum(lam_q1[l] * lam_k1[l])) - jnp.exp(jnp.sum(lam_q2[l] * lam_k2[l])) + lam_init)
        g_b = jnp.tile(diff_subln_g[l] * (1.0 - lam_init), LANES // B_V_DIM)
        g_b = jnp.broadcast_to(g_b[:, None], (LANES, _ATT_TQ))
        ob = _diff_call(qkv_b, lam.reshape(1), g_b)
        g_c = jnp.pad(mla_out_g[l].reshape(C_HEADS, C_V_DIM), ((0, 0), (0, LANES - C_V_DIM))).reshape(C_W)
        g_c = jnp.broadcast_to(g_c[:, None], (C_W, _ATT_TQ))
        oc = _mla_call(qc, kc, vc, g_c)

        x1, h, logits = _out_call(oa, ob, oc, x, wo_a, wo_b, wo_c, ffn_norm_g[l], w_router[l])
        x = _expert_choice(h, logits, x1, w_gate[l], w_up[l], w_down[l], final_norm_g, l == depth - 1)
    return x
```

```python
import functools
import math

import numpy as np

import jax
import jax.numpy as jnp
from jax import lax
from jax.experimental import pallas as pl
from jax.experimental.pallas import tpu as pltpu

F32 = jnp.float32
BF16 = jnp.bfloat16
LANES = 128

D_MODEL = 1024
HEAD_DIM = 64
A_HEADS = 6
A_DILATIONS = (1, 4, 16)
A_HALF_WINDOW = 64
B_HEADS = 5
B_QK_DIM = 32
B_V_DIM = 64
C_HEADS = 5
C_Q_RANK = 192
C_KV_RANK = 128
C_NOPE_DIM = 64
C_ROPE_DIM = 32
C_V_DIM = 64
C_ROPE_THETA = 10000.0
ROPE_THETA = 500000.0
ROPE_FRACTION = 4
N_EXPERTS = 16
CAPACITY_FACTOR = 2
EXPERT_FF = 1408
NORM_EPS = 1e-6
SUBLN_EPS = 1e-5
NEG_INF = -1e30
LOG2E = math.log2(math.e)

A_W = A_HEADS * HEAD_DIM
B_SLABS = 3
B_W = B_SLABS * LANES
C_W = C_HEADS * LANES
CQ_PAD = 256
PROJ_W = 3 * A_W + 3 * B_W + CQ_PAD + LANES + LANES
N_TABS = 9

VMEM_LIMIT = 56 * 1024 * 1024

_NT = (((1,), (1,)), ((), ()))


def _cparams(sem):
    return pltpu.CompilerParams(dimension_semantics=sem, vmem_limit_bytes=VMEM_LIMIT)


_ROPE_LAYOUTS = ((HEAD_DIM // ROPE_FRACTION, ROPE_THETA, HEAD_DIM, 0),
                 (B_QK_DIM // ROPE_FRACTION, ROPE_THETA, B_QK_DIM, 0),
                 (C_ROPE_DIM, C_ROPE_THETA, LANES, C_NOPE_DIM))
_ROPE_COLS = LANES // 2
_ROPE_ONE = sum(r for r, _, _, _ in _ROPE_LAYOUTS)


def _rope_compact(positions):
    pos = positions.astype(F32)
    inv = jnp.concatenate([1.0 / (theta ** (jnp.arange(0, rot_dim, 2, dtype=F32) / rot_dim))
                           for rot_dim, theta, _, _ in _ROPE_LAYOUTS])
    ang = inv[None, :, None] * pos[:, None, :]
    bsz, seq = pos.shape
    rows = jnp.concatenate([jnp.cos(ang), jnp.sin(ang), jnp.ones((bsz, 1, seq), F32),
                            jnp.zeros((bsz, _ROPE_COLS - _ROPE_ONE - 1, seq), F32)], axis=1)
    cs = jnp.swapaxes(rows, 1, 2)
    return jnp.concatenate([cs, cs], axis=-1)


def _rope_expansion():
    e = np.zeros((_ROPE_COLS, N_TABS * LANES), np.float32)
    col = 0
    for li, (rot_dim, _, period, offset) in enumerate(_ROPE_LAYOUTS):
        half = rot_dim // 2
        cos0, sin0 = col, _ROPE_ONE // 2 + col
        col += half
        for lane in range(LANES):
            d = lane % period - offset
            c_t, s1_t, s2_t = (3 * li) * LANES + lane, (3 * li + 1) * LANES + lane, (3 * li + 2) * LANES + lane
            if 0 <= d < half:
                e[cos0 + d, c_t] = 1.0
                e[sin0 + d, s1_t] = -1.0
            elif half <= d < 2 * half:
                e[cos0 + d - half, c_t] = 1.0
                e[sin0 + d - half, s2_t] = 1.0
            else:
                e[_ROPE_ONE, c_t] = 1.0
    return jnp.asarray(np.concatenate([e, e], axis=0), BF16)


def _pad_cols(w, width):
    return jnp.pad(w, ((0, 0), (0, width - w.shape[1])))


def _prep_layer(w_in, mla_q_norm_g, mla_w_uq, mla_kv_norm_g, mla_w_ukv, w_out):
    a = A_W
    bqk = B_HEADS * 2 * B_QK_DIM
    bv = B_HEADS * B_V_DIM
    o = 0
    aq = w_in[:, o:o + a] * (HEAD_DIM ** -0.5); o += a
    ak = w_in[:, o:o + a]; o += a
    av = w_in[:, o:o + a]; o += a
    bq = w_in[:, o:o + bqk] * (B_QK_DIM ** -0.5 * LOG2E); o += bqk
    bk = w_in[:, o:o + bqk]; o += bqk
    bvv = w_in[:, o:o + bv]; o += bv
    cq = w_in[:, o:o + C_Q_RANK]; o += C_Q_RANK
    ckv = w_in[:, o:o + C_KV_RANK]; o += C_KV_RANK
    ckr = w_in[:, o:o + C_ROPE_DIM]
    ckr_slab = jnp.pad(ckr, ((0, 0), (C_NOPE_DIM, LANES - C_NOPE_DIM - C_ROPE_DIM)))
    w_proj = jnp.concatenate([aq, ak, av, _pad_cols(bq, B_W), _pad_cols(bk, B_W), _pad_cols(bvv, B_W),
                              _pad_cols(cq, CQ_PAD), ckv, ckr_slab], axis=1).astype(BF16)

    qd = C_NOPE_DIM + C_ROPE_DIM
    wuq = mla_w_uq.reshape(C_Q_RANK, C_HEADS, qd) * (qd ** -0.5 * LOG2E)
    wuq = jnp.pad(wuq, ((0, CQ_PAD - C_Q_RANK), (0, 0), (0, LANES - qd))).reshape(CQ_PAD, C_W).astype(BF16)
    wukv = mla_w_ukv.reshape(C_KV_RANK, C_HEADS, C_NOPE_DIM + C_V_DIM)
    wk = jnp.pad(wukv[:, :, :C_NOPE_DIM], ((0, 0), (0, 0), (0, LANES - C_NOPE_DIM))).reshape(C_KV_RANK, C_W)
    wv = jnp.pad(wukv[:, :, C_NOPE_DIM:], ((0, 0), (0, 0), (0, LANES - C_V_DIM))).reshape(C_KV_RANK, C_W)
    wukv_p = jnp.concatenate([wk, wv], axis=1).astype(BF16)
    gq = jnp.pad(mla_q_norm_g, (0, CQ_PAD - C_Q_RANK)).reshape(1, CQ_PAD)
    gkv = mla_kv_norm_g.reshape(1, C_KV_RANK)

    wo_a = w_out[:a].astype(BF16)
    wo_b = jnp.pad(w_out[a:a + bv], ((0, B_W - bv), (0, 0))).astype(BF16)
    wo_c = w_out[a + bv:].reshape(C_HEADS, C_V_DIM, D_MODEL)
    wo_c = jnp.pad(wo_c, ((0, 0), (0, LANES - C_V_DIM), (0, 0))).reshape(C_W, D_MODEL).astype(BF16)
    return w_proj, gq, wuq, gkv, wukv_p, wo_a, wo_b, wo_c


def _proj_kernel(x_ref, g_ref, w_ref, cs_ref, exp_ref, gq_ref, wuq_ref, gkv_ref, wukv_ref,
                 qa_ref, qb_ref, qc_ref, kc_ref, vc_ref):
    x = x_ref[0]
    ms = jnp.mean(x * x, axis=-1, keepdims=True)
    h = (x * lax.rsqrt(ms + NORM_EPS) * g_ref[...]).astype(BF16)
    p = jnp.dot(h, w_ref[...], preferred_element_type=F32)

    cs = cs_ref[0]
    cs_hi = cs.astype(BF16)
    cs_lo = (cs - cs_hi.astype(F32)).astype(BF16)
    lane = lax.broadcasted_iota(jnp.int32, cs.shape, 1)
    tabs = jnp.dot(jnp.where(lane < _ROPE_COLS, cs_hi, cs_lo), exp_ref[...], preferred_element_type=F32)

    def slab(i):
        return p[:, i * LANES:(i + 1) * LANES]

    def rope(xs, layout, shift):
        c = tabs[:, (3 * layout) * LANES:(3 * layout + 1) * LANES]
        s1 = tabs[:, (3 * layout + 1) * LANES:(3 * layout + 2) * LANES]
        s2 = tabs[:, (3 * layout + 2) * LANES:(3 * layout + 3) * LANES]
        return xs * c + pltpu.roll(xs, LANES - shift, 1) * s1 + pltpu.roll(xs, shift, 1) * s2

    n_a = A_W // LANES
    for i in range(3 * n_a):
        v = slab(i)
        if i < 2 * n_a:
            v = rope(v, 0, HEAD_DIM // ROPE_FRACTION // 2)
        qa_ref[0, :, i * LANES:(i + 1) * LANES] = v
    base = 3 * n_a
    for i in range(3 * B_SLABS):
        v = slab(base + i)
        if i < 2 * B_SLABS:
            v = rope(v, 1, B_QK_DIM // ROPE_FRACTION // 2)
        qb_ref[0, :, i * LANES:(i + 1) * LANES] = v.astype(BF16)
    base += 3 * B_SLABS

    cq = p[:, base * LANES:base * LANES + CQ_PAD]
    msq = jnp.sum(cq * cq, axis=-1, keepdims=True) * (1.0 / C_Q_RANK)
    cqn = (cq * lax.rsqrt(msq + NORM_EPS) * gq_ref[...]).astype(BF16)
    qf = jnp.dot(cqn, wuq_ref[...], preferred_element_type=F32)
    base += CQ_PAD // LANES
    ckv = slab(base)
    mskv = jnp.mean(ckv * ckv, axis=-1, keepdims=True)
    ckvn = (ckv * lax.rsqrt(mskv + NORM_EPS) * gkv_ref[...]).astype(BF16)
    kvf = jnp.dot(ckvn, wukv_ref[...], preferred_element_type=F32)
    kr = rope(slab(base + 1), 2, C_ROPE_DIM // 2)
    for hd in range(C_HEADS):
        sl = slice(hd * LANES, (hd + 1) * LANES)
        qc_ref[0, :, sl] = rope(qf[:, sl], 2, C_ROPE_DIM // 2).astype(BF16)
        kc_ref[0, :, sl] = (kvf[:, sl] + kr).astype(BF16)
        vc_ref[0, :, sl] = kvf[:, C_W + hd * LANES:C_W + (hd + 1) * LANES].astype(BF16)


def _proj_call(x, g, w_proj, cs, expand, gq, wuq, gkv, wukv, tm=512):
    bsz, seq, d = x.shape
    grid = (bsz, seq // tm)
    row = lambda b, i: (b, i, 0)
    const = lambda b, i: (0, 0)
    out_w = (3 * A_W, 3 * B_W, C_W, C_W, C_W)
    out_dt = (F32, BF16, BF16, BF16, BF16)
    return pl.pallas_call(
        _proj_kernel,
        grid=grid,
        in_specs=[pl.BlockSpec((1, tm, d), row),
                  pl.BlockSpec((1, d), const),
                  pl.BlockSpec((d, PROJ_W), const),
                  pl.BlockSpec((1, tm, LANES), row),
                  pl.BlockSpec((LANES, N_TABS * LANES), const),
                  pl.BlockSpec((1, CQ_PAD), const),
                  pl.BlockSpec((CQ_PAD, C_W), const),
                  pl.BlockSpec((1, C_KV_RANK), const),
                  pl.BlockSpec((C_KV_RANK, 2 * C_W), const)],
        out_specs=[pl.BlockSpec((1, tm, w), row) for w in out_w],
        out_shape=[jax.ShapeDtypeStruct((bsz, seq, w), dt) for w, dt in zip(out_w, out_dt)],
        compiler_params=_cparams(("parallel", "parallel")),
        name="proj",
    )(x, g.reshape(1, d), w_proj, cs, expand, gq, wuq, gkv, wukv)


def _head_rmsnorm_t(ot, g_t, eps):
    ss = jnp.sum(ot * ot, axis=0, keepdims=True) * (1.0 / HEAD_DIM)
    return ot * lax.rsqrt(ss + eps) * g_t


def _head_rmsnorm(o, g, eps):
    ri = lax.broadcasted_iota(jnp.int32, (LANES, LANES), 0) // HEAD_DIM
    ci = lax.broadcasted_iota(jnp.int32, (LANES, LANES), 1) // HEAD_DIM
    bd = jnp.where(ri == ci, 1.0, 0.0).astype(BF16)
    sq = o * o
    sq_hi = sq.astype(BF16)
    sq_lo = (sq - sq_hi.astype(F32)).astype(BF16)
    ss = (jnp.dot(sq_hi, bd, preferred_element_type=F32) + jnp.dot(sq_lo, bd, preferred_element_type=F32))
    return o * lax.rsqrt(ss * (1.0 / HEAD_DIM) + eps) * g


_ATT_TQ = 256
_ATT_TK = 512


def _transpose_v(v_ref, vt_scr):
    for c in range(v_ref.shape[1] // _ATT_TK):
        sl = slice(c * _ATT_TK, (c + 1) * _ATT_TK)
        vt_scr[:, sl] = jnp.transpose(v_ref[0, sl, :].astype(F32)).astype(BF16)


def _scores_phase(qm, k_ref, s_scr, m_scr, slot):
    tq = qm.shape[0]
    mx = jnp.full((8, tq), -jnp.inf, F32)
    for c in range(k_ref.shape[1] // _ATT_TK):
        s = lax.dot_general(k_ref[0, c * _ATT_TK:(c + 1) * _ATT_TK, :], qm, _NT, preferred_element_type=F32)
        s_scr[slot, c] = s
        mx = jnp.maximum(mx, jnp.max(s.reshape(_ATT_TK // 8, 8, tq), axis=0))
    m_scr[slot] = mx


def _softmax_phase(vt_scr, rows, s_scr, m_scr, slot):
    m = jnp.max(m_scr[slot], axis=0, keepdims=True)
    tq = m.shape[1]
    lp = jnp.zeros((8, tq), F32)
    acc = jnp.zeros((rows.stop - rows.start, tq), F32)
    for c in range(s_scr.shape[1]):
        p = jnp.exp2(s_scr[slot, c] - m)
        lp = lp + jnp.sum(p.reshape(_ATT_TK // 8, 8, tq), axis=0)
        acc = acc + jnp.dot(vt_scr[rows, c * _ATT_TK:(c + 1) * _ATT_TK], p.astype(BF16),
                            preferred_element_type=F32)
    return acc / jnp.sum(lp, axis=0, keepdims=True)


def _attn_jobs(jobs, q_next, first, k_ref, vt_scr, rows, s_scr, m_scr):
    assert len(jobs) % 2 == 0

    @pl.when(first)
    def _():
        _scores_phase(jobs[0], k_ref, s_scr, m_scr, 0)

    outs = []
    for n in range(len(jobs)):
        nxt = jobs[n + 1] if n + 1 < len(jobs) else q_next
        _scores_phase(nxt, k_ref, s_scr, m_scr, (n + 1) % 2)
        outs.append(_softmax_phase(vt_scr, rows[n], s_scr, m_scr, n % 2))
    return outs


_DIL_TQ = 128
_DIL_TW = 256


def _dil_kernel(q_ref, k_ref, v_ref, g_ref, o_ref, *scr):
    seq = q_ref.shape[1]
    lane = lax.broadcasted_iota(jnp.int32, (1, LANES), 1)
    head0 = lane < HEAD_DIM
    rel = (lax.broadcasted_iota(jnp.int32, (_DIL_TQ, _DIL_TW), 1)
           - lax.broadcasted_iota(jnp.int32, (_DIL_TQ, _DIL_TW), 0))

    for pi, r in enumerate(A_DILATIONS):
        length = seq // r
        nblk = length // _DIL_TQ
        o_scr, m_scr, l_scr = scr[3 * pi:3 * pi + 3]

        def body(t, carry, r=r, length=length, nblk=nblk, o_scr=o_scr, m_scr=m_scr, l_scr=l_scr):
            c = t // nblk
            i = t % nblk
            ws = jnp.clip(i * _DIL_TQ - A_HALF_WINDOW, 0, length - _DIL_TW)
            qs = c + r * (i * _DIL_TQ)
            ks = c + r * ws
            if r == 1:
                qsl = pl.ds(pl.multiple_of(qs, _DIL_TQ), _DIL_TQ)
                ksl = pl.ds(pl.multiple_of(ks, A_HALF_WINDOW), _DIL_TW)
            else:
                qsl = pl.ds(qs, _DIL_TQ, stride=r)
                ksl = pl.ds(ks, _DIL_TW, stride=r)
            qb = q_ref[0, qsl, :]
            kb = k_ref[0, ksl, :].astype(BF16)
            vb = v_ref[0, ksl, :].astype(BF16)
            delta = rel + (ws - i * _DIL_TQ)
            valid = jnp.abs(delta) <= A_HALF_WINDOW
            res = []
            for hd in range(2):
                qm = jnp.where(head0 if hd == 0 else jnp.logical_not(head0), qb, 0.0).astype(BF16)
                s = lax.dot_general(qm, kb, _NT, preferred_element_type=F32)
                s = jnp.where(valid, s, NEG_INF)
                m = jnp.max(s, axis=1, keepdims=True)
                p = jnp.exp(s - m)
                l = jnp.sum(p, axis=1, keepdims=True)
                o = jnp.dot(p.astype(BF16), vb, preferred_element_type=F32)
                res.append((o, m, l))
            o_scr[qsl, :] = jnp.where(head0, res[0][0], res[1][0])
            m_scr[qsl, :] = jnp.where(head0, res[0][1], res[1][1])
            l_scr[qsl, :] = jnp.where(head0, res[0][2], res[1][2])
            return carry

        lax.fori_loop(0, r * nblk, body, 0, unroll=8)

    rows = 256
    g = g_ref[...]

    def combine(t, carry):
        sl = pl.ds(pl.multiple_of(t * rows, rows), rows)
        ms = [scr[3 * pi + 1][sl, :] for pi in range(3)]
        mm = jnp.maximum(jnp.maximum(ms[0], ms[1]), ms[2])
        num = jnp.zeros((rows, LANES), F32)
        den = jnp.zeros((rows, LANES), F32)
        for pi in range(3):
            e = jnp.exp(ms[pi] - mm)
            num = num + e * scr[3 * pi][sl, :]
            den = den + e * scr[3 * pi + 2][sl, :]
        o_ref[0, sl, :] = _head_rmsnorm(num / den, g, NORM_EPS).astype(o_ref.dtype)
        return carry

    lax.fori_loop(0, seq // rows, combine, 0)


def _dil_call(qkv_a, g):
    bsz, seq, _ = qkv_a.shape
    n_slab = A_W // LANES
    blk = (1, seq, LANES)
    return pl.pallas_call(
        _dil_kernel,
        grid=(bsz, n_slab),
        in_specs=[pl.BlockSpec(blk, lambda b, j: (b, 0, j)),
                  pl.BlockSpec(blk, lambda b, j: (b, 0, n_slab + j)),
                  pl.BlockSpec(blk, lambda b, j: (b, 0, 2 * n_slab + j)),
                  pl.BlockSpec((1, LANES), lambda b, j: (0, j))],
        out_specs=pl.BlockSpec(blk, lambda b, j: (b, 0, j)),
        out_shape=jax.ShapeDtypeStruct((bsz, seq, A_W), BF16),
        scratch_shapes=[pltpu.VMEM((seq, LANES), F32) for _ in range(9)],
        compiler_params=_cparams(("parallel", "parallel")),
        name="dilated_attn",
    )(qkv_a, qkv_a, qkv_a, g.reshape(1, A_W))


def _attn_scratch(seq):
    return [pltpu.VMEM((LANES, seq), BF16),
            pltpu.VMEM((2, seq // _ATT_TK, _ATT_TK, _ATT_TQ), F32),
            pltpu.VMEM((2, 8, _ATT_TQ), F32)]


_DIFF_TILES = 2


def _diff_kernel(lam_ref, q_ref, qn_ref, k_ref, v_ref, g_ref, o_ref, vt_scr, s_scr, m_scr):
    first = pl.program_id(2) == 0
    lane = lax.broadcasted_iota(jnp.int32, (1, LANES), 1)
    lam = lam_ref[0]

    def qmap(q, mp):
        lo = mp * B_QK_DIM
        return jnp.where((lane >= lo) & (lane < lo + B_QK_DIM), q, jnp.zeros_like(q))

    @pl.when(first)
    def _():
        _transpose_v(v_ref, vt_scr)

    def run(n_heads):
        nm = 2 * n_heads
        jobs = [qmap(q_ref[0, t * _ATT_TQ:(t + 1) * _ATT_TQ, :], mp)
                for t in range(_DIFF_TILES) for mp in range(nm)]
        rows = [slice((mp // 2) * HEAD_DIM, (mp // 2 + 1) * HEAD_DIM) for mp in range(nm)] * _DIFF_TILES
        outs = _attn_jobs(jobs, qmap(qn_ref[0], 0), first, k_ref, vt_scr, rows, s_scr, m_scr)
        for t in range(_DIFF_TILES):
            heads = [_head_rmsnorm_t(outs[t * nm + 2 * hd] - lam * outs[t * nm + 2 * hd + 1],
                                     g_ref[hd * HEAD_DIM:(hd + 1) * HEAD_DIM, :], SUBLN_EPS)
                     for hd in range(n_heads)]
            if n_heads == 1:
                heads.append(jnp.zeros_like(heads[0]))
            o_ref[0, t * _ATT_TQ:(t + 1) * _ATT_TQ, :] = jnp.transpose(
                jnp.concatenate(heads, axis=0)).astype(o_ref.dtype)

    @pl.when(pl.program_id(1) < B_HEADS // 2)
    def _():
        run(2)

    @pl.when(pl.program_id(1) == B_HEADS // 2)
    def _():
        run(1)


def _diff_call(qkv_b, lam, g):
    bsz, seq, _ = qkv_b.shape
    nq = seq // (_DIFF_TILES * _ATT_TQ)
    kv = (1, seq, LANES)
    last = seq // _ATT_TQ - 1
    return pl.pallas_call(
        _diff_kernel,
        grid=(bsz, B_SLABS, nq),
        in_specs=[pl.BlockSpec(memory_space=pltpu.SMEM),
                  pl.BlockSpec((1, _DIFF_TILES * _ATT_TQ, LANES), lambda b, j, i: (b, i, j)),
                  pl.BlockSpec((1, _ATT_TQ, LANES),
                               lambda b, j, i: (b, jnp.minimum(_DIFF_TILES * (i + 1), last), j)),
                  pl.BlockSpec(kv, lambda b, j, i: (b, 0, B_SLABS + j)),
                  pl.BlockSpec(kv, lambda b, j, i: (b, 0, 2 * B_SLABS + j)),
                  pl.BlockSpec((LANES, _ATT_TQ), lambda b, j, i: (0, 0))],
        out_specs=pl.BlockSpec((1, _DIFF_TILES * _ATT_TQ, LANES), lambda b, j, i: (b, i, j)),
        out_shape=jax.ShapeDtypeStruct((bsz, seq, B_W), BF16),
        scratch_shapes=_attn_scratch(seq),
        compiler_params=_cparams(("parallel", "parallel", "arbitrary")),
        name="diff_attn",
    )(lam, qkv_b, qkv_b, qkv_b, qkv_b, g)


_MLA_TILES = 4


def _mla_kernel(q_ref, qn_ref, k_ref, v_ref, g_ref, o_ref, vt_scr, s_scr, m_scr):
    first = pl.program_id(2) == 0

    @pl.when(first)
    def _():
        _transpose_v(v_ref, vt_scr)

    jobs = [q_ref[0, n * _ATT_TQ:(n + 1) * _ATT_TQ, :] for n in range(_MLA_TILES)]
    rows = [slice(0, C_V_DIM)] * _MLA_TILES
    outs = _attn_jobs(jobs, qn_ref[0], first, k_ref, vt_scr, rows, s_scr, m_scr)
    for n, o in enumerate(outs):
        on = _head_rmsnorm_t(o, g_ref[:C_V_DIM, :], NORM_EPS)
        ot = jnp.concatenate([on, jnp.zeros_like(on)], axis=0)
        o_ref[0, n * _ATT_TQ:(n + 1) * _ATT_TQ, :] = jnp.transpose(ot).astype(o_ref.dtype)


def _mla_call(qc, kc, vc, g):
    bsz, seq, _ = qc.shape
    nq = seq // (_MLA_TILES * _ATT_TQ)
    kv = (1, seq, LANES)
    last = seq // _ATT_TQ - 1
    return pl.pallas_call(
        _mla_kernel,
        grid=(bsz, C_HEADS, nq),
        in_specs=[pl.BlockSpec((1, _MLA_TILES * _ATT_TQ, LANES), lambda b, j, i: (b, i, j)),
                  pl.BlockSpec((1, _ATT_TQ, LANES),
                               lambda b, j, i: (b, jnp.minimum(_MLA_TILES * (i + 1), last), j)),
                  pl.BlockSpec(kv, lambda b, j, i: (b, 0, j)),
                  pl.BlockSpec(kv, lambda b, j, i: (b, 0, j)),
                  pl.BlockSpec((LANES, _ATT_TQ), lambda b, j, i: (j, 0))],
        out_specs=pl.BlockSpec((1, _MLA_TILES * _ATT_TQ, LANES), lambda b, j, i: (b, i, j)),
        out_shape=jax.ShapeDtypeStruct((bsz, seq, C_W), BF16),
        scratch_shapes=_attn_scratch(seq),
        compiler_params=_cparams(("parallel", "parallel", "arbitrary")),
        name="latent_attn",
    )(qc, qc, kc, vc, g)


def _out_kernel(oa_ref, ob_ref, oc_ref, x_ref, wa_ref, wb_ref, wc_ref, g_ref, wr_ref,
                x1_ref, h_ref, lg_ref):
    y = jnp.dot(oa_ref[0], wa_ref[...], preferred_element_type=F32)
    y = y + jnp.dot(ob_ref[0], wb_ref[...], preferred_element_type=F32)
    y = y + jnp.dot(oc_ref[0], wc_ref[...], preferred_element_type=F32)
    x1 = x_ref[0] + y
    x1_ref[0] = x1
    ms = jnp.mean(x1 * x1, axis=-1, keepdims=True)
    h = x1 * lax.rsqrt(ms + NORM_EPS) * g_ref[...]
    h_hi = h.astype(BF16)
    h_ref[0] = h_hi
    h_lo = (h - h_hi.astype(F32)).astype(BF16)
    t = jnp.dot(h_hi, wr_ref[...], preferred_element_type=F32)
    lg_ref[0] = (t[:, :LANES] + t[:, LANES:]
                 + jnp.dot(h_lo, wr_ref[:, :LANES], preferred_element_type=F32))


def _out_call(oa, ob, oc, x, wo_a, wo_b, wo_c, g, w_router, tm=512):
    bsz, seq, d = x.shape
    row = lambda b, i: (b, i, 0)
    const = lambda b, i: (0, 0)
    wr = _pad_cols(w_router, LANES)
    wr_hi = wr.astype(BF16)
    wr = jnp.concatenate([wr_hi, (wr - wr_hi.astype(F32)).astype(BF16)], axis=1)
    return pl.pallas_call(
        _out_kernel,
        grid=(bsz, seq // tm),
        in_specs=[pl.BlockSpec((1, tm, A_W), row),
                  pl.BlockSpec((1, tm, B_W), row),
                  pl.BlockSpec((1, tm, C_W), row),
                  pl.BlockSpec((1, tm, d), row),
                  pl.BlockSpec((A_W, d), const),
                  pl.BlockSpec((B_W, d), const),
                  pl.BlockSpec((C_W, d), const),
                  pl.BlockSpec((1, d), const),
                  pl.BlockSpec((d, 2 * LANES), const)],
        out_specs=[pl.BlockSpec((1, tm, d), row),
                   pl.BlockSpec((1, tm, d), row),
                   pl.BlockSpec((1, tm, LANES), row)],
        out_shape=[jax.ShapeDtypeStruct((bsz, seq, d), F32),
                   jax.ShapeDtypeStruct((bsz, seq, d), BF16),
                   jax.ShapeDtypeStruct((bsz, seq, LANES), F32)],
        compiler_params=_cparams(("parallel", "parallel")),
        name="out_proj",
    )(oa, ob, oc, x, wo_a, wo_b, wo_c, g.reshape(1, d), wr)


_MOE_CHUNK = 256


def _router_kernel(lg_ref, pos_ref, post_ref, aff_ref, bnd_ref, pre_scr, *, cap):
    seq = lg_ref.shape[1]
    lt = jnp.transpose(lg_ref[0])[:N_EXPERTS]
    ex = jnp.exp(lt - jnp.max(lt, axis=0, keepdims=True))
    aff = ex / jnp.sum(ex, axis=0, keepdims=True)
    bits = pltpu.bitcast(aff, jnp.int32)

    def step(k, v):
        cand = v | lax.shift_left(jnp.int32(1), 30 - k)
        cnt = jnp.sum((bits >= cand).astype(jnp.int32), axis=1, keepdims=True)
        return jnp.where(cnt >= cap, cand, v)

    kth = lax.fori_loop(0, 31, step, jnp.zeros((N_EXPERTS, 1), jnp.int32))
    gt = bits > kth
    eq = bits == kth
    need = (cap - jnp.sum(gt.astype(jnp.int32), axis=1, keepdims=True)).astype(F32)

    flags = jnp.concatenate([gt, eq], axis=0)
    tri = (lax.broadcasted_iota(jnp.int32, (LANES, LANES), 0)
           <= lax.broadcasted_iota(jnp.int32, (LANES, LANES), 1)).astype(BF16)
    carry = jnp.zeros((2 * N_EXPERTS, 1), F32)
    bnd_ref[0] = jnp.zeros((N_EXPERTS, LANES), F32)
    for j in range(seq // LANES):
        if j % (_MOE_CHUNK // LANES) == 0:
            t = j // (_MOE_CHUNK // LANES)
            bnd_ref[0, :, t:t + 1] = carry[:N_EXPERTS] + jnp.minimum(carry[N_EXPERTS:], need)
        blk = jnp.where(flags[:, j * LANES:(j + 1) * LANES], 1.0, 0.0)
        inc = jnp.dot(blk.astype(BF16), tri, preferred_element_type=F32)
        pre_scr[:, j * LANES:(j + 1) * LANES] = inc - blk + carry
        carry = carry + inc[:, LANES - 1:LANES]
    n_chunks = seq // _MOE_CHUNK
    bnd_ref[0, :, n_chunks:n_chunks + 1] = carry[:N_EXPERTS] + jnp.minimum(carry[N_EXPERTS:], need)
    pre_gt = pre_scr[:N_EXPERTS]
    pre_eq = pre_scr[N_EXPERTS:]
    sel = jnp.logical_or(gt, jnp.logical_and(eq, pre_eq < need))
    pos = jnp.where(sel, pre_gt + jnp.minimum(pre_eq, need), -1.0)
    for c in range(n_chunks):
        sl = slice(c * _MOE_CHUNK, (c + 1) * _MOE_CHUNK)
        pos_ref[0, c] = pos[:, sl].astype(jnp.int32)
        aff_ref[0, c] = aff[:, sl]
    pad = jnp.full((LANES - N_EXPERTS, seq), -1.0, F32)
    post_ref[0] = jnp.transpose(jnp.concatenate([pos, pad], axis=0))


def _router_call(logits, cap):
    bsz, seq, _ = logits.shape
    chunked = (1, seq // _MOE_CHUNK, N_EXPERTS, _MOE_CHUNK)
    return pl.pallas_call(
        functools.partial(_router_kernel, cap=cap),
        grid=(bsz,),
        in_specs=[pl.BlockSpec((1, seq, LANES), lambda b: (b, 0, 0))],
        out_specs=[pl.BlockSpec(chunked, lambda b: (b, 0, 0, 0)),
                   pl.BlockSpec((1, seq, LANES), lambda b: (b, 0, 0)),
                   pl.BlockSpec(chunked, lambda b: (b, 0, 0, 0)),
                   pl.BlockSpec((1, N_EXPERTS, LANES), lambda b: (b, 0, 0))],
        out_shape=[jax.ShapeDtypeStruct((bsz,) + chunked[1:], jnp.int32),
                   jax.ShapeDtypeStruct((bsz, seq, LANES), F32),
                   jax.ShapeDtypeStruct((bsz,) + chunked[1:], F32),
                   jax.ShapeDtypeStruct((bsz, N_EXPERTS, LANES), F32)],
        scratch_shapes=[pltpu.VMEM((2 * N_EXPERTS, seq), F32)],
        compiler_params=_cparams(("parallel",)),
        name="router",
    )(logits)


def _gather_kernel(pos_ref, aff_ref, h_ref, xe_ref, gate_ref, p_scr, *, cap):
    e = pl.program_id(1)
    slot = lax.broadcasted_iota(jnp.int32, (cap, _MOE_CHUNK), 0)
    gate = jnp.zeros((cap, 1), F32)
    for c in range(pos_ref.shape[1]):
        hit = pos_ref[0, c, pl.ds(e, 1), :] == slot
        p_scr[:, c * _MOE_CHUNK:(c + 1) * _MOE_CHUNK] = jnp.where(hit, 1.0, 0.0).astype(BF16)
        gate = gate + jnp.sum(jnp.where(hit, aff_ref[0, c, pl.ds(e, 1), :], 0.0), axis=1, keepdims=True)
    xe_ref[0] = jnp.dot(p_scr[...], h_ref[0], preferred_element_type=F32).astype(BF16)
    gate_ref[0] = gate


def _gather_call(pos, aff, h_bf16, cap):
    bsz, seq, d = h_bf16.shape
    chunked = (1, seq // _MOE_CHUNK, N_EXPERTS, _MOE_CHUNK)
    return pl.pallas_call(
        functools.partial(_gather_kernel, cap=cap),
        grid=(bsz, N_EXPERTS),
        in_specs=[pl.BlockSpec(chunked, lambda b, e: (b, 0, 0, 0)),
                  pl.BlockSpec(chunked, lambda b, e: (b, 0, 0, 0)),
                  pl.BlockSpec((1, seq, d), lambda b, e: (b, 0, 0))],
        out_specs=[pl.BlockSpec((1, cap, d), lambda b, e: (e, b, 0)),
                   pl.BlockSpec((1, cap, 1), lambda b, e: (e, b, 0))],
        out_shape=[jax.ShapeDtypeStruct((N_EXPERTS, bsz * cap, d), BF16),
                   jax.ShapeDtypeStruct((N_EXPERTS, bsz * cap, 1), F32)],
        scratch_shapes=[pltpu.VMEM((cap, seq), BF16)],
        compiler_params=_cparams(("parallel", "arbitrary")),
        name="moe_gather",
    )(pos, aff, h_bf16)


def _expert_kernel(xe_ref, gate_ref, wg_ref, wu_ref, wd_ref, y_ref, wg_scr, wu_scr, wd_scr):
    @pl.when(pl.program_id(1) == 0)
    def _():
        wg_scr[...] = wg_ref[0].astype(BF16)
        wu_scr[...] = wu_ref[0].astype(BF16)
        wd_scr[...] = wd_ref[0].astype(BF16)

    xe = xe_ref[0]
    a = jnp.dot(xe, wg_scr[...], preferred_element_type=F32)
    u = jnp.dot(xe, wu_scr[...], preferred_element_type=F32)
    hmid = (a * jax.nn.sigmoid(a) * u).astype(BF16)
    y = jnp.dot(hmid, wd_scr[...], preferred_element_type=F32)
    y_ref[0] = (y * gate_ref[0]).astype(y_ref.dtype)


def _expert_call(xe, gate, w_gate, w_up, w_down, layer, cap):
    n_e, m, d = xe.shape
    bsz = m // cap
    first = layer * n_e
    return pl.pallas_call(
        _expert_kernel,
        grid=(n_e, bsz),
        in_specs=[pl.BlockSpec((1, cap, d), lambda e, b: (e, b, 0)),
                  pl.BlockSpec((1, cap, 1), lambda e, b: (e, b, 0)),
                  pl.BlockSpec((1, d, EXPERT_FF), lambda e, b: (first + e, 0, 0)),
                  pl.BlockSpec((1, d, EXPERT_FF), lambda e, b: (first + e, 0, 0)),
                  pl.BlockSpec((1, EXPERT_FF, d), lambda e, b: (first + e, 0, 0))],
        out_specs=pl.BlockSpec((1, cap, d), lambda e, b: (b, e, 0)),
        out_shape=jax.ShapeDtypeStruct((bsz, n_e * cap, d), BF16),
        scratch_shapes=[pltpu.VMEM((d, EXPERT_FF), BF16), pltpu.VMEM((d, EXPERT_FF), BF16),
                        pltpu.VMEM((EXPERT_FF, d), BF16)],
        compiler_params=_cparams(("parallel", "arbitrary")),
        name="experts",
    )(xe, gate, w_gate, w_up, w_down)


_SCATTER_WIN = 256
_WIN_ALIGN = 16


def _scatter_kernel(win_ref, post_ref, x1_ref, y_ref, g_ref, o_ref, *, cap, final_norm):
    base = ((pl.program_id(0) * pl.num_programs(1) + pl.program_id(1)) * N_EXPERTS) * 2
    slot = lax.broadcasted_iota(jnp.int32, (_MOE_CHUNK, _SCATTER_WIN), 1).astype(F32)

    def onehot(col):
        return jnp.where(col == slot, 1.0, 0.0).astype(BF16)

    acc = x1_ref[0]
    for e in range(N_EXPERTS):
        w0 = win_ref[base + 2 * e]
        col = post_ref[0, :, e:e + 1] - w0.astype(F32)
        rows = pl.ds(pl.multiple_of(e * cap + w0, _WIN_ALIGN), _SCATTER_WIN)
        acc = acc + jnp.dot(onehot(col), y_ref[0, rows, :], preferred_element_type=F32)
    o_ref[0] = acc

    tail = cap - _SCATTER_WIN
    for e in range(N_EXPERTS):
        @pl.when(win_ref[base + 2 * e + 1] != 0)
        def _(e=e):
            col = post_ref[0, :, e:e + 1]
            done = (win_ref[base + 2 * e] + _SCATTER_WIN).astype(F32)
            col = jnp.where(col >= done, col - tail, -1.0)
            o_ref[0] += jnp.dot(onehot(col), y_ref[0, e * cap + tail:(e + 1) * cap, :],
                                preferred_element_type=F32)

    if final_norm:
        x = o_ref[0]
        ms = jnp.mean(x * x, axis=-1, keepdims=True)
        o_ref[0] = x * lax.rsqrt(ms + NORM_EPS) * g_ref[...]


def _scatter_call(win, post, x1, y, g, cap, final_norm):
    bsz, seq, d = x1.shape
    assert cap <= 2 * _SCATTER_WIN
    return pl.pallas_call(
        functools.partial(_scatter_kernel, cap=cap, final_norm=final_norm),
        grid_spec=pltpu.PrefetchScalarGridSpec(
            num_scalar_prefetch=1,
            grid=(bsz, seq // _MOE_CHUNK),
            in_specs=[pl.BlockSpec((1, _MOE_CHUNK, LANES), lambda b, t, win: (b, t, 0)),
                      pl.BlockSpec((1, _MOE_CHUNK, d), lambda b, t, win: (b, t, 0)),
                      pl.BlockSpec((1, N_EXPERTS * cap, d), lambda b, t, win: (b, 0, 0)),
                      pl.BlockSpec((1, d), lambda b, t, win: (0, 0))],
            out_specs=pl.BlockSpec((1, _MOE_CHUNK, d), lambda b, t, win: (b, t, 0))),
        out_shape=jax.ShapeDtypeStruct((bsz, seq, d), F32),
        compiler_params=_cparams(("parallel", "arbitrary")),
        name="moe_scatter",
    )(win, post, x1, y, g.reshape(1, d))


def _expert_choice(h, logits, x1, w_gate, w_up, w_down, layer, final_g, final_norm):
    seq = h.shape[1]
    cap = CAPACITY_FACTOR * seq // N_EXPERTS
    pos, post, aff, bnd = _router_call(logits, cap)
    bnd = bnd.astype(jnp.int32)
    n_chunks = seq // _MOE_CHUNK
    start, stop = bnd[:, :, :n_chunks], bnd[:, :, 1:n_chunks + 1]
    w0 = jnp.minimum(start // _WIN_ALIGN * _WIN_ALIGN, cap - _SCATTER_WIN)
    win = jnp.stack([w0, (stop > w0 + _SCATTER_WIN).astype(jnp.int32)], axis=-1)
    win = jnp.swapaxes(win, 1, 2)
    xe, gate = _gather_call(pos, aff, h, cap)
    y = _expert_call(xe, gate, w_gate, w_up, w_down, layer, cap)
    return _scatter_call(win.reshape(-1), post, x1, y, final_g, cap, final_norm)


def kernel(x, positions, attn_norm_g, w_in, lam_q1, lam_k1, lam_q2, lam_k2, diff_subln_g, mla_q_norm_g, mla_w_uq, mla_kv_norm_g, mla_w_ukv, dil_out_g, mla_out_g, w_out, ffn_norm_g, w_router, w_gate, w_up, w_down, final_norm_g):
    depth = w_in.shape[0]
    wg_all = w_gate.reshape((depth * N_EXPERTS,) + w_gate.shape[2:])
    wu_all = w_up.reshape((depth * N_EXPERTS,) + w_up.shape[2:])
    wd_all = w_down.reshape((depth * N_EXPERTS,) + w_down.shape[2:])
    cs = _rope_compact(positions)
    expand = _rope_expansion()
    for l in range(depth):
        lam_init = 0.8 - 0.6 * math.exp(-0.3 * l)
        w_proj, gq, wuq, gkv, wukv, wo_a, wo_b, wo_c = _prep_layer(
            w_in[l], mla_q_norm_g[l], mla_w_uq[l], mla_kv_norm_g[l], mla_w_ukv[l], w_out[l])
        qkv_a, qkv_b, qc, kc, vc = _proj_call(x, attn_norm_g[l], w_proj, cs, expand, gq, wuq, gkv, wukv)

        oa = _dil_call(qkv_a, dil_out_g[l])
        lam = (jnp.exp(jnp.sum(lam_q1[l] * lam_k1[l])) - jnp.exp(jnp.sum(lam_q2[l] * lam_k2[l])) + lam_init)
        g_b = jnp.tile(diff_subln_g[l] * (1.0 - lam_init), LANES // B_V_DIM)
        g_b = jnp.broadcast_to(g_b[:, None], (LANES, _ATT_TQ))
        ob = _diff_call(qkv_b, lam.reshape(1), g_b)
        g_c = jnp.pad(mla_out_g[l].reshape(C_HEADS, C_V_DIM), ((0, 0), (0, LANES - C_V_DIM))).reshape(C_W)
        g_c = jnp.broadcast_to(g_c[:, None], (C_W, _ATT_TQ))
        oc = _mla_call(qc, kc, vc, g_c)

        x1, h, logits = _out_call(oa, ob, oc, x, wo_a, wo_b, wo_c, ffn_norm_g[l], w_router[l])
        x = _expert_choice(h, logits, x1, wg_all, wu_all, wd_all, l, final_norm_g, l == depth - 1)
    return x
```

```python
import functools
import math

import numpy as np

import jax
import jax.numpy as jnp
from jax import lax
from jax.experimental import pallas as pl
from jax.experimental.pallas import tpu as pltpu

F32 = jnp.float32
BF16 = jnp.bfloat16
LANES = 128

D_MODEL = 1024
HEAD_DIM = 64
A_HEADS = 6
A_DILATIONS = (1, 4, 16)
A_HALF_WINDOW = 64
B_HEADS = 5
B_QK_DIM = 32
B_V_DIM = 64
C_HEADS = 5
C_Q_RANK = 192
C_KV_RANK = 128
C_NOPE_DIM = 64
C_ROPE_DIM = 32
C_V_DIM = 64
C_ROPE_THETA = 10000.0
ROPE_THETA = 500000.0
ROPE_FRACTION = 4
N_EXPERTS = 16
CAPACITY_FACTOR = 2
EXPERT_FF = 1408
NORM_EPS = 1e-6
SUBLN_EPS = 1e-5
NEG_INF = -1e30
LOG2E = math.log2(math.e)

A_W = A_HEADS * HEAD_DIM
B_SLABS = 3
B_W = B_SLABS * LANES
C_W = C_HEADS * LANES
CQ_PAD = 256
PROJ_W = 3 * A_W + 3 * B_W + CQ_PAD + LANES + LANES
N_TABS = 9

VMEM_LIMIT = 56 * 1024 * 1024

_NT = (((1,), (1,)), ((), ()))


def _cparams(sem):
    return pltpu.CompilerParams(dimension_semantics=sem, vmem_limit_bytes=VMEM_LIMIT)


_ROPE_LAYOUTS = ((HEAD_DIM // ROPE_FRACTION, ROPE_THETA, HEAD_DIM, 0),
                 (B_QK_DIM // ROPE_FRACTION, ROPE_THETA, B_QK_DIM, 0),
                 (C_ROPE_DIM, C_ROPE_THETA, LANES, C_NOPE_DIM))
_ROPE_COLS = LANES // 2
_ROPE_ONE = sum(r for r, _, _, _ in _ROPE_LAYOUTS)


def _rope_compact(positions):
    pos = positions.astype(F32)
    inv = jnp.concatenate([1.0 / (theta ** (jnp.arange(0, rot_dim, 2, dtype=F32) / rot_dim))
                           for rot_dim, theta, _, _ in _ROPE_LAYOUTS])
    ang = inv[None, :, None] * pos[:, None, :]
    bsz, seq = pos.shape
    rows = jnp.concatenate([jnp.cos(ang), jnp.sin(ang), jnp.ones((bsz, 1, seq), F32),
                            jnp.zeros((bsz, _ROPE_COLS - _ROPE_ONE - 1, seq), F32)], axis=1)
    cs = jnp.swapaxes(rows, 1, 2)
    return jnp.concatenate([cs, cs], axis=-1)


def _rope_expansion():
    e = np.zeros((_ROPE_COLS, N_TABS * LANES), np.float32)
    col = 0
    for li, (rot_dim, _, period, offset) in enumerate(_ROPE_LAYOUTS):
        half = rot_dim // 2
        cos0, sin0 = col, _ROPE_ONE // 2 + col
        col += half
        for lane in range(LANES):
            d = lane % period - offset
            c_t, s1_t, s2_t = (3 * li) * LANES + lane, (3 * li + 1) * LANES + lane, (3 * li + 2) * LANES + lane
            if 0 <= d < half:
                e[cos0 + d, c_t] = 1.0
                e[sin0 + d, s1_t] = -1.0
            elif half <= d < 2 * half:
                e[cos0 + d - half, c_t] = 1.0
                e[sin0 + d - half, s2_t] = 1.0
            else:
                e[_ROPE_ONE, c_t] = 1.0
    return jnp.asarray(np.concatenate([e, e], axis=0), BF16)


def _pad_cols(w, width):
    return jnp.pad(w, ((0, 0), (0, width - w.shape[1])))


def _prep_layer(w_in, mla_q_norm_g, mla_w_uq, mla_kv_norm_g, mla_w_ukv, w_out):
    a = A_W
    bqk = B_HEADS * 2 * B_QK_DIM
    bv = B_HEADS * B_V_DIM
    o = 0
    aq = w_in[:, o:o + a] * (HEAD_DIM ** -0.5); o += a
    ak = w_in[:, o:o + a]; o += a
    av = w_in[:, o:o + a]; o += a
    bq = w_in[:, o:o + bqk] * (B_QK_DIM ** -0.5 * LOG2E); o += bqk
    bk = w_in[:, o:o + bqk]; o += bqk
    bvv = w_in[:, o:o + bv]; o += bv
    cq = w_in[:, o:o + C_Q_RANK]; o += C_Q_RANK
    ckv = w_in[:, o:o + C_KV_RANK]; o += C_KV_RANK
    ckr = w_in[:, o:o + C_ROPE_DIM]
    ckr_slab = jnp.pad(ckr, ((0, 0), (C_NOPE_DIM, LANES - C_NOPE_DIM - C_ROPE_DIM)))
    w_proj = jnp.concatenate([aq, ak, av, _pad_cols(bq, B_W), _pad_cols(bk, B_W), _pad_cols(bvv, B_W),
                              _pad_cols(cq, CQ_PAD), ckv, ckr_slab], axis=1).astype(BF16)

    qd = C_NOPE_DIM + C_ROPE_DIM
    wuq = mla_w_uq.reshape(C_Q_RANK, C_HEADS, qd) * (qd ** -0.5 * LOG2E)
    wuq = jnp.pad(wuq, ((0, CQ_PAD - C_Q_RANK), (0, 0), (0, LANES - qd))).reshape(CQ_PAD, C_W).astype(BF16)
    wukv = mla_w_ukv.reshape(C_KV_RANK, C_HEADS, C_NOPE_DIM + C_V_DIM)
    wk = jnp.pad(wukv[:, :, :C_NOPE_DIM], ((0, 0), (0, 0), (0, LANES - C_NOPE_DIM))).reshape(C_KV_RANK, C_W)
    wv = jnp.pad(wukv[:, :, C_NOPE_DIM:], ((0, 0), (0, 0), (0, LANES - C_V_DIM))).reshape(C_KV_RANK, C_W)
    wukv_p = jnp.concatenate([wk, wv], axis=1).astype(BF16)
    gq = jnp.pad(mla_q_norm_g, (0, CQ_PAD - C_Q_RANK)).reshape(1, CQ_PAD)
    gkv = mla_kv_norm_g.reshape(1, C_KV_RANK)

    wo_a = w_out[:a].astype(BF16)
    wo_b = jnp.pad(w_out[a:a + bv], ((0, B_W - bv), (0, 0))).astype(BF16)
    wo_c = w_out[a + bv:].reshape(C_HEADS, C_V_DIM, D_MODEL)
    wo_c = jnp.pad(wo_c, ((0, 0), (0, LANES - C_V_DIM), (0, 0))).reshape(C_W, D_MODEL).astype(BF16)
    return w_proj, gq, wuq, gkv, wukv_p, wo_a, wo_b, wo_c


def _proj_kernel(x_ref, g_ref, w_ref, cs_ref, exp_ref, gq_ref, wuq_ref, gkv_ref, wukv_ref,
                 qa_ref, qb_ref, qc_ref, kc_ref, vc_ref):
    x = x_ref[0]
    ms = jnp.mean(x * x, axis=-1, keepdims=True)
    h = (x * lax.rsqrt(ms + NORM_EPS) * g_ref[...]).astype(BF16)
    p = jnp.dot(h, w_ref[...], preferred_element_type=F32)

    cs = cs_ref[0]
    cs_hi = cs.astype(BF16)
    cs_lo = (cs - cs_hi.astype(F32)).astype(BF16)
    lane = lax.broadcasted_iota(jnp.int32, cs.shape, 1)
    tabs = jnp.dot(jnp.where(lane < _ROPE_COLS, cs_hi, cs_lo), exp_ref[...], preferred_element_type=F32)

    def slab(i):
        return p[:, i * LANES:(i + 1) * LANES]

    def rope(xs, layout, shift):
        c = tabs[:, (3 * layout) * LANES:(3 * layout + 1) * LANES]
        s1 = tabs[:, (3 * layout + 1) * LANES:(3 * layout + 2) * LANES]
        s2 = tabs[:, (3 * layout + 2) * LANES:(3 * layout + 3) * LANES]
        return xs * c + pltpu.roll(xs, LANES - shift, 1) * s1 + pltpu.roll(xs, shift, 1) * s2

    n_a = A_W // LANES
    for i in range(3 * n_a):
        v = slab(i)
        if i < 2 * n_a:
            v = rope(v, 0, HEAD_DIM // ROPE_FRACTION // 2)
        qa_ref[0, :, i * LANES:(i + 1) * LANES] = v
    base = 3 * n_a
    for i in range(3 * B_SLABS):
        v = slab(base + i)
        if i < 2 * B_SLABS:
            v = rope(v, 1, B_QK_DIM // ROPE_FRACTION // 2)
        qb_ref[0, :, i * LANES:(i + 1) * LANES] = v.astype(BF16)
    base += 3 * B_SLABS

    cq = p[:, base * LANES:base * LANES + CQ_PAD]
    msq = jnp.sum(cq * cq, axis=-1, keepdims=True) * (1.0 / C_Q_RANK)
    cqn = (cq * lax.rsqrt(msq + NORM_EPS) * gq_ref[...]).astype(BF16)
    qf = jnp.dot(cqn, wuq_ref[...], preferred_element_type=F32)
    base += CQ_PAD // LANES
    ckv = slab(base)
    mskv = jnp.mean(ckv * ckv, axis=-1, keepdims=True)
    ckvn = (ckv * lax.rsqrt(mskv + NORM_EPS) * gkv_ref[...]).astype(BF16)
    kvf = jnp.dot(ckvn, wukv_ref[...], preferred_element_type=F32)
    kr = rope(slab(base + 1), 2, C_ROPE_DIM // 2)
    for hd in range(C_HEADS):
        sl = slice(hd * LANES, (hd + 1) * LANES)
        qc_ref[0, :, sl] = rope(qf[:, sl], 2, C_ROPE_DIM // 2).astype(BF16)
        kc_ref[0, :, sl] = (kvf[:, sl] + kr).astype(BF16)
        vc_ref[0, :, sl] = kvf[:, C_W + hd * LANES:C_W + (hd + 1) * LANES].astype(BF16)


def _proj_call(x, g, w_proj, cs, expand, gq, wuq, gkv, wukv, tm=512):
    bsz, seq, d = x.shape
    grid = (bsz, seq // tm)
    row = lambda b, i: (b, i, 0)
    const = lambda b, i: (0, 0)
    out_w = (3 * A_W, 3 * B_W, C_W, C_W, C_W)
    out_dt = (F32, BF16, BF16, BF16, BF16)
    return pl.pallas_call(
        _proj_kernel,
        grid=grid,
        in_specs=[pl.BlockSpec((1, tm, d), row),
                  pl.BlockSpec((1, d), const),
                  pl.BlockSpec((d, PROJ_W), const),
                  pl.BlockSpec((1, tm, LANES), row),
                  pl.BlockSpec((LANES, N_TABS * LANES), const),
                  pl.BlockSpec((1, CQ_PAD), const),
                  pl.BlockSpec((CQ_PAD, C_W), const),
                  pl.BlockSpec((1, C_KV_RANK), const),
                  pl.BlockSpec((C_KV_RANK, 2 * C_W), const)],
        out_specs=[pl.BlockSpec((1, tm, w), row) for w in out_w],
        out_shape=[jax.ShapeDtypeStruct((bsz, seq, w), dt) for w, dt in zip(out_w, out_dt)],
        compiler_params=_cparams(("parallel", "parallel")),
        name="proj",
    )(x, g.reshape(1, d), w_proj, cs, expand, gq, wuq, gkv, wukv)


def _head_rmsnorm_t(ot, g_t, eps):
    ss = jnp.sum(ot * ot, axis=0, keepdims=True) * (1.0 / HEAD_DIM)
    return ot * lax.rsqrt(ss + eps) * g_t


def _head_rmsnorm(o, g, eps):
    ri = lax.broadcasted_iota(jnp.int32, (LANES, LANES), 0) // HEAD_DIM
    ci = lax.broadcasted_iota(jnp.int32, (LANES, LANES), 1) // HEAD_DIM
    bd = jnp.where(ri == ci, 1.0, 0.0).astype(BF16)
    sq = o * o
    sq_hi = sq.astype(BF16)
    sq_lo = (sq - sq_hi.astype(F32)).astype(BF16)
    ss = (jnp.dot(sq_hi, bd, preferred_element_type=F32) + jnp.dot(sq_lo, bd, preferred_element_type=F32))
    return o * lax.rsqrt(ss * (1.0 / HEAD_DIM) + eps) * g


_ATT_TQ = 256
_ATT_TK = 512


def _transpose_v(v_ref, vt_scr):
    for c in range(v_ref.shape[1] // _ATT_TK):
        sl = slice(c * _ATT_TK, (c + 1) * _ATT_TK)
        vt_scr[:, sl] = jnp.transpose(v_ref[0, sl, :].astype(F32)).astype(BF16)


def _scores_phase(qm, k_ref, s_scr, m_scr, slot):
    tq = qm.shape[0]
    mx = jnp.full((8, tq), -jnp.inf, F32)
    for c in range(k_ref.shape[1] // _ATT_TK):
        s = lax.dot_general(k_ref[0, c * _ATT_TK:(c + 1) * _ATT_TK, :], qm, _NT, preferred_element_type=F32)
        s_scr[slot, c] = s
        mx = jnp.maximum(mx, jnp.max(s.reshape(_ATT_TK // 8, 8, tq), axis=0))
    m_scr[slot] = mx


def _softmax_phase(vt_scr, rows, s_scr, m_scr, slot):
    m = jnp.max(m_scr[slot], axis=0, keepdims=True)
    tq = m.shape[1]
    lp = jnp.zeros((8, tq), F32)
    acc = jnp.zeros((rows.stop - rows.start, tq), F32)
    for c in range(s_scr.shape[1]):
        p = jnp.exp2(s_scr[slot, c] - m)
        lp = lp + jnp.sum(p.reshape(_ATT_TK // 8, 8, tq), axis=0)
        acc = acc + jnp.dot(vt_scr[rows, c * _ATT_TK:(c + 1) * _ATT_TK], p.astype(BF16),
                            preferred_element_type=F32)
    return acc / jnp.sum(lp, axis=0, keepdims=True)


def _attn_jobs(jobs, q_next, first, k_ref, vt_scr, rows, s_scr, m_scr):
    assert len(jobs) % 2 == 0

    @pl.when(first)
    def _():
        _scores_phase(jobs[0], k_ref, s_scr, m_scr, 0)

    outs = []
    for n in range(len(jobs)):
        nxt = jobs[n + 1] if n + 1 < len(jobs) else q_next
        _scores_phase(nxt, k_ref, s_scr, m_scr, (n + 1) % 2)
        outs.append(_softmax_phase(vt_scr, rows[n], s_scr, m_scr, n % 2))
    return outs


_DIL_TQ = 128
_DIL_TW = 256


def _dil_kernel(q_ref, k_ref, v_ref, g_ref, o_ref, *scr):
    seq = q_ref.shape[1]
    lane = lax.broadcasted_iota(jnp.int32, (1, LANES), 1)
    head0 = lane < HEAD_DIM
    rel = (lax.broadcasted_iota(jnp.int32, (_DIL_TQ, _DIL_TW), 1)
           - lax.broadcasted_iota(jnp.int32, (_DIL_TQ, _DIL_TW), 0))

    for pi, r in enumerate(A_DILATIONS):
        length = seq // r
        nblk = length // _DIL_TQ
        o_scr, m_scr, l_scr = scr[3 * pi:3 * pi + 3]

        def body(t, carry, r=r, length=length, nblk=nblk, o_scr=o_scr, m_scr=m_scr, l_scr=l_scr):
            c = t // nblk
            i = t % nblk
            ws = jnp.clip(i * _DIL_TQ - A_HALF_WINDOW, 0, length - _DIL_TW)
            qs = c + r * (i * _DIL_TQ)
            ks = c + r * ws
            if r == 1:
                qsl = pl.ds(pl.multiple_of(qs, _DIL_TQ), _DIL_TQ)
                ksl = pl.ds(pl.multiple_of(ks, A_HALF_WINDOW), _DIL_TW)
            else:
                qsl = pl.ds(qs, _DIL_TQ, stride=r)
                ksl = pl.ds(ks, _DIL_TW, stride=r)
            qb = q_ref[0, qsl, :]
            kb = k_ref[0, ksl, :].astype(BF16)
            vb = v_ref[0, ksl, :].astype(BF16)
            delta = rel + (ws - i * _DIL_TQ)
            valid = jnp.abs(delta) <= A_HALF_WINDOW
            res = []
            for hd in range(2):
                qm = jnp.where(head0 if hd == 0 else jnp.logical_not(head0), qb, 0.0).astype(BF16)
                s = lax.dot_general(qm, kb, _NT, preferred_element_type=F32)
                s = jnp.where(valid, s, NEG_INF)
                m = jnp.max(s, axis=1, keepdims=True)
                p = jnp.exp(s - m)
                l = jnp.sum(p, axis=1, keepdims=True)
                o = jnp.dot(p.astype(BF16), vb, preferred_element_type=F32)
                res.append((o, m, l))
            o_scr[qsl, :] = jnp.where(head0, res[0][0], res[1][0])
            m_scr[qsl, :] = jnp.where(head0, res[0][1], res[1][1])
            l_scr[qsl, :] = jnp.where(head0, res[0][2], res[1][2])
            return carry

        lax.fori_loop(0, r * nblk, body, 0, unroll=8)

    rows = 256
    g = g_ref[...]

    def combine(t, carry):
        sl = pl.ds(pl.multiple_of(t * rows, rows), rows)
        ms = [scr[3 * pi + 1][sl, :] for pi in range(3)]
        mm = jnp.maximum(jnp.maximum(ms[0], ms[1]), ms[2])
        num = jnp.zeros((rows, LANES), F32)
        den = jnp.zeros((rows, LANES), F32)
        for pi in range(3):
            e = jnp.exp(ms[pi] - mm)
            num = num + e * scr[3 * pi][sl, :]
            den = den + e * scr[3 * pi + 2][sl, :]
        o_ref[0, sl, :] = _head_rmsnorm(num / den, g, NORM_EPS).astype(o_ref.dtype)
        return carry

    lax.fori_loop(0, seq // rows, combine, 0)


def _dil_call(qkv_a, g):
    bsz, seq, _ = qkv_a.shape
    n_slab = A_W // LANES
    blk = (1, seq, LANES)
    return pl.pallas_call(
        _dil_kernel,
        grid=(bsz, n_slab),
        in_specs=[pl.BlockSpec(blk, lambda b, j: (b, 0, j)),
                  pl.BlockSpec(blk, lambda b, j: (b, 0, n_slab + j)),
                  pl.BlockSpec(blk, lambda b, j: (b, 0, 2 * n_slab + j)),
                  pl.BlockSpec((1, LANES), lambda b, j: (0, j))],
        out_specs=pl.BlockSpec(blk, lambda b, j: (b, 0, j)),
        out_shape=jax.ShapeDtypeStruct((bsz, seq, A_W), BF16),
        scratch_shapes=[pltpu.VMEM((seq, LANES), F32) for _ in range(9)],
        compiler_params=_cparams(("parallel", "parallel")),
        name="dilated_attn",
    )(qkv_a, qkv_a, qkv_a, g.reshape(1, A_W))


def _attn_scratch(seq):
    return [pltpu.VMEM((LANES, seq), BF16),
            pltpu.VMEM((2, seq // _ATT_TK, _ATT_TK, _ATT_TQ), F32),
            pltpu.VMEM((2, 8, _ATT_TQ), F32)]


_DIFF_TILES = 2


def _diff_kernel(lam_ref, q_ref, qn_ref, k_ref, v_ref, g_ref, o_ref, vt_scr, s_scr, m_scr):
    first = pl.program_id(2) == 0
    lane = lax.broadcasted_iota(jnp.int32, (1, LANES), 1)
    lam = lam_ref[0]

    def qmap(q, mp):
        lo = mp * B_QK_DIM
        return jnp.where((lane >= lo) & (lane < lo + B_QK_DIM), q, jnp.zeros_like(q))

    @pl.when(first)
    def _():
        _transpose_v(v_ref, vt_scr)

    def run(n_heads):
        nm = 2 * n_heads
        jobs = [qmap(q_ref[0, t * _ATT_TQ:(t + 1) * _ATT_TQ, :], mp)
                for t in range(_DIFF_TILES) for mp in range(nm)]
        rows = [slice((mp // 2) * HEAD_DIM, (mp // 2 + 1) * HEAD_DIM) for mp in range(nm)] * _DIFF_TILES
        outs = _attn_jobs(jobs, qmap(qn_ref[0], 0), first, k_ref, vt_scr, rows, s_scr, m_scr)
        for t in range(_DIFF_TILES):
            heads = [_head_rmsnorm_t(outs[t * nm + 2 * hd] - lam * outs[t * nm + 2 * hd + 1],
                                     g_ref[hd * HEAD_DIM:(hd + 1) * HEAD_DIM, :], SUBLN_EPS)
                     for hd in range(n_heads)]
            if n_heads == 1:
                heads.append(jnp.zeros_like(heads[0]))
            o_ref[0, t * _ATT_TQ:(t + 1) * _ATT_TQ, :] = jnp.transpose(
                jnp.concatenate(heads, axis=0)).astype(o_ref.dtype)

    @pl.when(pl.program_id(1) < B_HEADS // 2)
    def _():
        run(2)

    @pl.when(pl.program_id(1) == B_HEADS // 2)
    def _():
        run(1)


def _diff_call(qkv_b, lam, g):
    bsz, seq, _ = qkv_b.shape
    nq = seq // (_DIFF_TILES * _ATT_TQ)
    kv = (1, seq, LANES)
    last = seq // _ATT_TQ - 1
    return pl.pallas_call(
        _diff_kernel,
        grid=(bsz, B_SLABS, nq),
        in_specs=[pl.BlockSpec(memory_space=pltpu.SMEM),
                  pl.BlockSpec((1, _DIFF_TILES * _ATT_TQ, LANES), lambda b, j, i: (b, i, j)),
                  pl.BlockSpec((1, _ATT_TQ, LANES),
                               lambda b, j, i: (b, jnp.minimum(_DIFF_TILES * (i + 1), last), j)),
                  pl.BlockSpec(kv, lambda b, j, i: (b, 0, B_SLABS + j)),
                  pl.BlockSpec(kv, lambda b, j, i: (b, 0, 2 * B_SLABS + j)),
                  pl.BlockSpec((LANES, _ATT_TQ), lambda b, j, i: (0, 0))],
        out_specs=pl.BlockSpec((1, _DIFF_TILES * _ATT_TQ, LANES), lambda b, j, i: (b, i, j)),
        out_shape=jax.ShapeDtypeStruct((bsz, seq, B_W), BF16),
        scratch_shapes=_attn_scratch(seq),
        compiler_params=_cparams(("parallel", "parallel", "arbitrary")),
        name="diff_attn",
    )(lam, qkv_b, qkv_b, qkv_b, qkv_b, g)


_MLA_TILES = 8


def _mla_kernel(q_ref, qn_ref, k_ref, v_ref, g_ref, o_ref, vt_scr, s_scr, m_scr):
    first = pl.program_id(2) == 0

    @pl.when(first)
    def _():
        _transpose_v(v_ref, vt_scr)

    jobs = [q_ref[0, n * _ATT_TQ:(n + 1) * _ATT_TQ, :] for n in range(_MLA_TILES)]
    rows = [slice(0, C_V_DIM)] * _MLA_TILES
    outs = _attn_jobs(jobs, qn_ref[0], first, k_ref, vt_scr, rows, s_scr, m_scr)
    for n, o in enumerate(outs):
        on = _head_rmsnorm_t(o, g_ref[:C_V_DIM, :], NORM_EPS)
        ot = jnp.concatenate([on, jnp.zeros_like(on)], axis=0)
        o_ref[0, n * _ATT_TQ:(n + 1) * _ATT_TQ, :] = jnp.transpose(ot).astype(o_ref.dtype)


def _mla_call(qc, kc, vc, g):
    bsz, seq, _ = qc.shape
    nq = seq // (_MLA_TILES * _ATT_TQ)
    kv = (1, seq, LANES)
    last = seq // _ATT_TQ - 1
    return pl.pallas_call(
        _mla_kernel,
        grid=(bsz, C_HEADS, nq),
        in_specs=[pl.BlockSpec((1, _MLA_TILES * _ATT_TQ, LANES), lambda b, j, i: (b, i, j)),
                  pl.BlockSpec((1, _ATT_TQ, LANES),
                               lambda b, j, i: (b, jnp.minimum(_MLA_TILES * (i + 1), last), j)),
                  pl.BlockSpec(kv, lambda b, j, i: (b, 0, j)),
                  pl.BlockSpec(kv, lambda b, j, i: (b, 0, j)),
                  pl.BlockSpec((LANES, _ATT_TQ), lambda b, j, i: (j, 0))],
        out_specs=pl.BlockSpec((1, _MLA_TILES * _ATT_TQ, LANES), lambda b, j, i: (b, i, j)),
        out_shape=jax.ShapeDtypeStruct((bsz, seq, C_W), BF16),
        scratch_shapes=_attn_scratch(seq),
        compiler_params=_cparams(("parallel", "parallel", "arbitrary")),
        name="latent_attn",
    )(qc, qc, kc, vc, g)


def _out_kernel(oa_ref, ob_ref, oc_ref, x_ref, wa_ref, wb_ref, wc_ref, g_ref, wr_ref,
                x1_ref, h_ref, lg_ref):
    y = jnp.dot(oa_ref[0], wa_ref[...], preferred_element_type=F32)
    y = y + jnp.dot(ob_ref[0], wb_ref[...], preferred_element_type=F32)
    y = y + jnp.dot(oc_ref[0], wc_ref[...], preferred_element_type=F32)
    x1 = x_ref[0] + y
    x1_ref[0] = x1
    ms = jnp.mean(x1 * x1, axis=-1, keepdims=True)
    h = x1 * lax.rsqrt(ms + NORM_EPS) * g_ref[...]
    h_hi = h.astype(BF16)
    h_ref[0] = h_hi
    h_lo = (h - h_hi.astype(F32)).astype(BF16)
    t = jnp.dot(h_hi, wr_ref[...], preferred_element_type=F32)
    lg_ref[0] = (t[:, :LANES] + t[:, LANES:]
                 + jnp.dot(h_lo, wr_ref[:, :LANES], preferred_element_type=F32))


def _out_call(oa, ob, oc, x, wo_a, wo_b, wo_c, g, w_router, tm=512):
    bsz, seq, d = x.shape
    row = lambda b, i: (b, i, 0)
    const = lambda b, i: (0, 0)
    wr = _pad_cols(w_router, LANES)
    wr_hi = wr.astype(BF16)
    wr = jnp.concatenate([wr_hi, (wr - wr_hi.astype(F32)).astype(BF16)], axis=1)
    return pl.pallas_call(
        _out_kernel,
        grid=(bsz, seq // tm),
        in_specs=[pl.BlockSpec((1, tm, A_W), row),
                  pl.BlockSpec((1, tm, B_W), row),
                  pl.BlockSpec((1, tm, C_W), row),
                  pl.BlockSpec((1, tm, d), row),
                  pl.BlockSpec((A_W, d), const),
                  pl.BlockSpec((B_W, d), const),
                  pl.BlockSpec((C_W, d), const),
                  pl.BlockSpec((1, d), const),
                  pl.BlockSpec((d, 2 * LANES), const)],
        out_specs=[pl.BlockSpec((1, tm, d), row),
                   pl.BlockSpec((1, tm, d), row),
                   pl.BlockSpec((1, tm, LANES), row)],
        out_shape=[jax.ShapeDtypeStruct((bsz, seq, d), F32),
                   jax.ShapeDtypeStruct((bsz, seq, d), BF16),
                   jax.ShapeDtypeStruct((bsz, seq, LANES), F32)],
        compiler_params=_cparams(("parallel", "parallel")),
        name="out_proj",
    )(oa, ob, oc, x, wo_a, wo_b, wo_c, g.reshape(1, d), wr)


_MOE_CHUNK = 256


def _router_kernel(lg_ref, pos_ref, post_ref, aff_ref, bnd_ref, pre_scr, *, cap):
    seq = lg_ref.shape[1]
    lt = jnp.transpose(lg_ref[0])[:N_EXPERTS]
    ex = jnp.exp(lt - jnp.max(lt, axis=0, keepdims=True))
    aff = ex / jnp.sum(ex, axis=0, keepdims=True)
    bits = pltpu.bitcast(aff, jnp.int32)

    def step(k, v):
        cand = v | lax.shift_left(jnp.int32(1), 30 - k)
        cnt = jnp.sum((bits >= cand).astype(jnp.int32), axis=1, keepdims=True)
        return jnp.where(cnt >= cap, cand, v)

    kth = lax.fori_loop(0, 31, step, jnp.zeros((N_EXPERTS, 1), jnp.int32))
    gt = bits > kth
    eq = bits == kth
    need = (cap - jnp.sum(gt.astype(jnp.int32), axis=1, keepdims=True)).astype(F32)

    flags = jnp.concatenate([gt, eq], axis=0)
    tri = (lax.broadcasted_iota(jnp.int32, (LANES, LANES), 0)
           <= lax.broadcasted_iota(jnp.int32, (LANES, LANES), 1)).astype(BF16)
    carry = jnp.zeros((2 * N_EXPERTS, 1), F32)
    bnd_ref[0] = jnp.zeros((N_EXPERTS, LANES), F32)
    for j in range(seq // LANES):
        if j % (_MOE_CHUNK // LANES) == 0:
            t = j // (_MOE_CHUNK // LANES)
            bnd_ref[0, :, t:t + 1] = carry[:N_EXPERTS] + jnp.minimum(carry[N_EXPERTS:], need)
        blk = jnp.where(flags[:, j * LANES:(j + 1) * LANES], 1.0, 0.0)
        inc = jnp.dot(blk.astype(BF16), tri, preferred_element_type=F32)
        pre_scr[:, j * LANES:(j + 1) * LANES] = inc - blk + carry
        carry = carry + inc[:, LANES - 1:LANES]
    n_chunks = seq // _MOE_CHUNK
    bnd_ref[0, :, n_chunks:n_chunks + 1] = carry[:N_EXPERTS] + jnp.minimum(carry[N_EXPERTS:], need)
    pre_gt = pre_scr[:N_EXPERTS]
    pre_eq = pre_scr[N_EXPERTS:]
    sel = jnp.logical_or(gt, jnp.logical_and(eq, pre_eq < need))
    pos = jnp.where(sel, pre_gt + jnp.minimum(pre_eq, need), -1.0)
    for c in range(n_chunks):
        sl = slice(c * _MOE_CHUNK, (c + 1) * _MOE_CHUNK)
        pos_ref[0, c] = pos[:, sl].astype(jnp.int32)
        aff_ref[0, c] = aff[:, sl]
    pad = jnp.full((LANES - N_EXPERTS, seq), -1.0, F32)
    post_ref[0] = jnp.transpose(jnp.concatenate([pos, pad], axis=0))


def _router_call(logits, cap):
    bsz, seq, _ = logits.shape
    chunked = (1, seq // _MOE_CHUNK, N_EXPERTS, _MOE_CHUNK)
    return pl.pallas_call(
        functools.partial(_router_kernel, cap=cap),
        grid=(bsz,),
        in_specs=[pl.BlockSpec((1, seq, LANES), lambda b: (b, 0, 0))],
        out_specs=[pl.BlockSpec(chunked, lambda b: (b, 0, 0, 0)),
                   pl.BlockSpec((1, seq, LANES), lambda b: (b, 0, 0)),
                   pl.BlockSpec(chunked, lambda b: (b, 0, 0, 0)),
                   pl.BlockSpec((1, N_EXPERTS, LANES), lambda b: (b, 0, 0))],
        out_shape=[jax.ShapeDtypeStruct((bsz,) + chunked[1:], jnp.int32),
                   jax.ShapeDtypeStruct((bsz, seq, LANES), F32),
                   jax.ShapeDtypeStruct((bsz,) + chunked[1:], F32),
                   jax.ShapeDtypeStruct((bsz, N_EXPERTS, LANES), F32)],
        scratch_shapes=[pltpu.VMEM((2 * N_EXPERTS, seq), F32)],
        compiler_params=_cparams(("parallel",)),
        name="router",
    )(logits)


def _gather_kernel(pos_ref, aff_ref, h_ref, xe_ref, gate_ref, p_scr, *, cap):
    e = pl.program_id(1)
    slot = lax.broadcasted_iota(jnp.int32, (cap, _MOE_CHUNK), 0)
    gate = jnp.zeros((cap, 1), F32)
    for c in range(pos_ref.shape[1]):
        hit = pos_ref[0, c, pl.ds(e, 1), :] == slot
        p_scr[:, c * _MOE_CHUNK:(c + 1) * _MOE_CHUNK] = jnp.where(hit, 1.0, 0.0).astype(BF16)
        gate = gate + jnp.sum(jnp.where(hit, aff_ref[0, c, pl.ds(e, 1), :], 0.0), axis=1, keepdims=True)
    xe_ref[0] = jnp.dot(p_scr[...], h_ref[0], preferred_element_type=F32).astype(BF16)
    gate_ref[0] = gate


def _gather_call(pos, aff, h_bf16, cap):
    bsz, seq, d = h_bf16.shape
    chunked = (1, seq // _MOE_CHUNK, N_EXPERTS, _MOE_CHUNK)
    return pl.pallas_call(
        functools.partial(_gather_kernel, cap=cap),
        grid=(bsz, N_EXPERTS),
        in_specs=[pl.BlockSpec(chunked, lambda b, e: (b, 0, 0, 0)),
                  pl.BlockSpec(chunked, lambda b, e: (b, 0, 0, 0)),
                  pl.BlockSpec((1, seq, d), lambda b, e: (b, 0, 0))],
        out_specs=[pl.BlockSpec((1, cap, d), lambda b, e: (e, b, 0)),
                   pl.BlockSpec((1, cap, 1), lambda b, e: (e, b, 0))],
        out_shape=[jax.ShapeDtypeStruct((N_EXPERTS, bsz * cap, d), BF16),
                   jax.ShapeDtypeStruct((N_EXPERTS, bsz * cap, 1), F32)],
        scratch_shapes=[pltpu.VMEM((cap, seq), BF16)],
        compiler_params=_cparams(("parallel", "arbitrary")),
        name="moe_gather",
    )(pos, aff, h_bf16)


def _expert_kernel(xe_ref, gate_ref, wg_ref, wu_ref, wd_ref, y_ref, wg_scr, wu_scr, wd_scr):
    @pl.when(pl.program_id(1) == 0)
    def _():
        wg_scr[...] = wg_ref[0].astype(BF16)
        wu_scr[...] = wu_ref[0].astype(BF16)
        wd_scr[...] = wd_ref[0].astype(BF16)

    xe = xe_ref[0]
    a = jnp.dot(xe, wg_scr[...], preferred_element_type=F32)
    u = jnp.dot(xe, wu_scr[...], preferred_element_type=F32)
    hmid = (a * jax.nn.sigmoid(a) * u).astype(BF16)
    y = jnp.dot(hmid, wd_scr[...], preferred_element_type=F32)
    y_ref[0] = (y * gate_ref[0]).astype(y_ref.dtype)


def _expert_call(xe, gate, w_gate, w_up, w_down, layer, cap):
    n_e, m, d = xe.shape
    bsz = m // cap
    first = layer * n_e
    return pl.pallas_call(
        _expert_kernel,
        grid=(n_e, bsz),
        in_specs=[pl.BlockSpec((1, cap, d), lambda e, b: (e, b, 0)),
                  pl.BlockSpec((1, cap, 1), lambda e, b: (e, b, 0)),
                  pl.BlockSpec((1, d, EXPERT_FF), lambda e, b: (first + e, 0, 0)),
                  pl.BlockSpec((1, d, EXPERT_FF), lambda e, b: (first + e, 0, 0)),
                  pl.BlockSpec((1, EXPERT_FF, d), lambda e, b: (first + e, 0, 0))],
        out_specs=pl.BlockSpec((1, cap, d), lambda e, b: (b, e, 0)),
        out_shape=jax.ShapeDtypeStruct((bsz, n_e * cap, d), BF16),
        scratch_shapes=[pltpu.VMEM((d, EXPERT_FF), BF16), pltpu.VMEM((d, EXPERT_FF), BF16),
                        pltpu.VMEM((EXPERT_FF, d), BF16)],
        compiler_params=_cparams(("parallel", "arbitrary")),
        name="experts",
    )(xe, gate, w_gate, w_up, w_down)


_SCATTER_WIN = 256
_WIN_ALIGN = 16


def _scatter_kernel(win_ref, post_ref, x1_ref, y_ref, g_ref, o_ref, *, cap, final_norm):
    base = ((pl.program_id(0) * pl.num_programs(1) + pl.program_id(1)) * N_EXPERTS) * 2
    slot = lax.broadcasted_iota(jnp.int32, (_MOE_CHUNK, _SCATTER_WIN), 1).astype(F32)

    def onehot(col):
        return jnp.where(col == slot, 1.0, 0.0).astype(BF16)

    acc = x1_ref[0]
    for e in range(N_EXPERTS):
        w0 = win_ref[base + 2 * e]
        col = post_ref[0, :, e:e + 1] - w0.astype(F32)
        rows = pl.ds(pl.multiple_of(e * cap + w0, _WIN_ALIGN), _SCATTER_WIN)
        acc = acc + jnp.dot(onehot(col), y_ref[0, rows, :], preferred_element_type=F32)
    o_ref[0] = acc

    tail = cap - _SCATTER_WIN
    for e in range(N_EXPERTS):
        @pl.when(win_ref[base + 2 * e + 1] != 0)
        def _(e=e):
            col = post_ref[0, :, e:e + 1]
            done = (win_ref[base + 2 * e] + _SCATTER_WIN).astype(F32)
            col = jnp.where(col >= done, col - tail, -1.0)
            o_ref[0] += jnp.dot(onehot(col), y_ref[0, e * cap + tail:(e + 1) * cap, :],
                                preferred_element_type=F32)

    if final_norm:
        x = o_ref[0]
        ms = jnp.mean(x * x, axis=-1, keepdims=True)
        o_ref[0] = x * lax.rsqrt(ms + NORM_EPS) * g_ref[...]


def _scatter_call(win, post, x1, y, g, cap, final_norm):
    bsz, seq, d = x1.shape
    assert cap <= 2 * _SCATTER_WIN
    return pl.pallas_call(
        functools.partial(_scatter_kernel, cap=cap, final_norm=final_norm),
        grid_spec=pltpu.PrefetchScalarGridSpec(
            num_scalar_prefetch=1,
            grid=(bsz, seq // _MOE_CHUNK),
            in_specs=[pl.BlockSpec((1, _MOE_CHUNK, LANES), lambda b, t, win: (b, t, 0)),
                      pl.BlockSpec((1, _MOE_CHUNK, d), lambda b, t, win: (b, t, 0)),
                      pl.BlockSpec((1, N_EXPERTS * cap, d), lambda b, t, win: (b, 0, 0)),
                      pl.BlockSpec((1, d), lambda b, t, win: (0, 0))],
            out_specs=pl.BlockSpec((1, _MOE_CHUNK, d), lambda b, t, win: (b, t, 0))),
        out_shape=jax.ShapeDtypeStruct((bsz, seq, d), F32),
        compiler_params=_cparams(("parallel", "arbitrary")),
        name="moe_scatter",
    )(win, post, x1, y, g.reshape(1, d))


def _expert_choice(h, logits, x1, w_gate, w_up, w_down, layer, final_g, final_norm):
    seq = h.shape[1]
    cap = CAPACITY_FACTOR * seq // N_EXPERTS
    pos, post, aff, bnd = _router_call(logits, cap)
    bnd = bnd.astype(jnp.int32)
    n_chunks = seq // _MOE_CHUNK
    start, stop = bnd[:, :, :n_chunks], bnd[:, :, 1:n_chunks + 1]
    w0 = jnp.minimum(start // _WIN_ALIGN * _WIN_ALIGN, cap - _SCATTER_WIN)
    win = jnp.stack([w0, (stop > w0 + _SCATTER_WIN).astype(jnp.int32)], axis=-1)
    win = jnp.swapaxes(win, 1, 2)
    xe, gate = _gather_call(pos, aff, h, cap)
    y = _expert_call(xe, gate, w_gate, w_up, w_down, layer, cap)
    return _scatter_call(win.reshape(-1), post, x1, y, final_g, cap, final_norm)


def kernel(x, positions, attn_norm_g, w_in, lam_q1, lam_k1, lam_q2, lam_k2, diff_subln_g, mla_q_norm_g, mla_w_uq, mla_kv_norm_g, mla_w_ukv, dil_out_g, mla_out_g, w_out, ffn_norm_g, w_router, w_gate, w_up, w_down, final_norm_g):
    depth = w_in.shape[0]
    wg_all = w_gate.reshape((depth * N_EXPERTS,) + w_gate.shape[2:])
    wu_all = w_up.reshape((depth * N_EXPERTS,) + w_up.shape[2:])
    wd_all = w_down.reshape((depth * N_EXPERTS,) + w_down.shape[2:])
    cs = _rope_compact(positions)
    expand = _rope_expansion()
    for l in range(depth):
        lam_init = 0.8 - 0.6 * math.exp(-0.3 * l)
        w_proj, gq, wuq, gkv, wukv, wo_a, wo_b, wo_c = _prep_layer(
            w_in[l], mla_q_norm_g[l], mla_w_uq[l], mla_kv_norm_g[l], mla_w_ukv[l], w_out[l])
        qkv_a, qkv_b, qc, kc, vc = _proj_call(x, attn_norm_g[l], w_proj, cs, expand, gq, wuq, gkv, wukv)

        oa = _dil_call(qkv_a, dil_out_g[l])
        lam = (jnp.exp(jnp.sum(lam_q1[l] * lam_k1[l])) - jnp.exp(jnp.sum(lam_q2[l] * lam_k2[l])) + lam_init)
        g_b = jnp.tile(diff_subln_g[l] * (1.0 - lam_init), LANES // B_V_DIM)
        g_b = jnp.broadcast_to(g_b[:, None], (LANES, _ATT_TQ))
        ob = _diff_call(qkv_b, lam.reshape(1), g_b)
        g_c = jnp.pad(mla_out_g[l].reshape(C_HEADS, C_V_DIM), ((0, 0), (0, LANES - C_V_DIM))).reshape(C_W)
        g_c = jnp.broadcast_to(g_c[:, None], (C_W, _ATT_TQ))
        oc = _mla_call(qc, kc, vc, g_c)

        x1, h, logits = _out_call(oa, ob, oc, x, wo_a, wo_b, wo_c, ffn_norm_g[l], w_router[l])
        x = _expert_choice(h, logits, x1, wg_all, wu_all, wd_all, l, final_norm_g, l == depth - 1)
    return x
```

```python
import functools
import math

import numpy as np

import jax
import jax.numpy as jnp
from jax import lax
from jax.experimental import pallas as pl
from jax.experimental.pallas import tpu as pltpu

F32 = jnp.float32
BF16 = jnp.bfloat16
LANES = 128

D_MODEL = 1024
HEAD_DIM = 64
A_HEADS = 6
A_DILATIONS = (1, 4, 16)
A_HALF_WINDOW = 64
B_HEADS = 5
B_QK_DIM = 32
B_V_DIM = 64
C_HEADS = 5
C_Q_RANK = 192
C_KV_RANK = 128
C_NOPE_DIM = 64
C_ROPE_DIM = 32
C_V_DIM = 64
C_ROPE_THETA = 10000.0
ROPE_THETA = 500000.0
ROPE_FRACTION = 4
N_EXPERTS = 16
CAPACITY_FACTOR = 2
EXPERT_FF = 1408
NORM_EPS = 1e-6
SUBLN_EPS = 1e-5
NEG_INF = -1e30
LOG2E = math.log2(math.e)

A_W = A_HEADS * HEAD_DIM
B_SLABS = 3
B_W = B_SLABS * LANES
C_W = C_HEADS * LANES
CQ_PAD = 256
PROJ_W = 3 * A_W + 3 * B_W + CQ_PAD + LANES + LANES
N_TABS = 9

VMEM_LIMIT = 56 * 1024 * 1024

_NT = (((1,), (1,)), ((), ()))


def _cparams(sem):
    return pltpu.CompilerParams(dimension_semantics=sem, vmem_limit_bytes=VMEM_LIMIT)


_ROPE_LAYOUTS = ((HEAD_DIM // ROPE_FRACTION, ROPE_THETA, HEAD_DIM, 0),
                 (B_QK_DIM // ROPE_FRACTION, ROPE_THETA, B_QK_DIM, 0),
                 (C_ROPE_DIM, C_ROPE_THETA, LANES, C_NOPE_DIM))
_ROPE_COLS = LANES // 2
_ROPE_ONE = sum(r for r, _, _, _ in _ROPE_LAYOUTS)


def _rope_compact(positions):
    pos = positions.astype(F32)
    inv = jnp.concatenate([1.0 / (theta ** (jnp.arange(0, rot_dim, 2, dtype=F32) / rot_dim))
                           for rot_dim, theta, _, _ in _ROPE_LAYOUTS])
    ang = inv[None, :, None] * pos[:, None, :]
    bsz, seq = pos.shape
    rows = jnp.concatenate([jnp.cos(ang), jnp.sin(ang), jnp.ones((bsz, 1, seq), F32),
                            jnp.zeros((bsz, _ROPE_COLS - _ROPE_ONE - 1, seq), F32)], axis=1)
    cs = jnp.swapaxes(rows, 1, 2)
    return jnp.concatenate([cs, cs], axis=-1)


def _rope_expansion():
    e = np.zeros((_ROPE_COLS, N_TABS * LANES), np.float32)
    col = 0
    for li, (rot_dim, _, period, offset) in enumerate(_ROPE_LAYOUTS):
        half = rot_dim // 2
        cos0, sin0 = col, _ROPE_ONE // 2 + col
        col += half
        for lane in range(LANES):
            d = lane % period - offset
            c_t, s1_t, s2_t = (3 * li) * LANES + lane, (3 * li + 1) * LANES + lane, (3 * li + 2) * LANES + lane
            if 0 <= d < half:
                e[cos0 + d, c_t] = 1.0
                e[sin0 + d, s1_t] = -1.0
            elif half <= d < 2 * half:
                e[cos0 + d - half, c_t] = 1.0
                e[sin0 + d - half, s2_t] = 1.0
            else:
                e[_ROPE_ONE, c_t] = 1.0
    return jnp.asarray(np.concatenate([e, e], axis=0), BF16)


def _pad_cols(w, width):
    return jnp.pad(w, ((0, 0), (0, width - w.shape[1])))


def _prep_layer(w_in, mla_q_norm_g, mla_w_uq, mla_kv_norm_g, mla_w_ukv, w_out):
    a = A_W
    bqk = B_HEADS * 2 * B_QK_DIM
    bv = B_HEADS * B_V_DIM
    o = 0
    aq = w_in[:, o:o + a] * (HEAD_DIM ** -0.5); o += a
    ak = w_in[:, o:o + a]; o += a
    av = w_in[:, o:o + a]; o += a
    bq = w_in[:, o:o + bqk] * (B_QK_DIM ** -0.5 * LOG2E); o += bqk
    bk = w_in[:, o:o + bqk]; o += bqk
    bvv = w_in[:, o:o + bv]; o += bv
    cq = w_in[:, o:o + C_Q_RANK]; o += C_Q_RANK
    ckv = w_in[:, o:o + C_KV_RANK]; o += C_KV_RANK
    ckr = w_in[:, o:o + C_ROPE_DIM]
    ckr_slab = jnp.pad(ckr, ((0, 0), (C_NOPE_DIM, LANES - C_NOPE_DIM - C_ROPE_DIM)))
    w_proj = jnp.concatenate([aq, ak, av, _pad_cols(bq, B_W), _pad_cols(bk, B_W), _pad_cols(bvv, B_W),
                              _pad_cols(cq, CQ_PAD), ckv, ckr_slab], axis=1).astype(BF16)

    qd = C_NOPE_DIM + C_ROPE_DIM
    wuq = mla_w_uq.reshape(C_Q_RANK, C_HEADS, qd) * (qd ** -0.5 * LOG2E)
    wuq = jnp.pad(wuq, ((0, CQ_PAD - C_Q_RANK), (0, 0), (0, LANES - qd))).reshape(CQ_PAD, C_W).astype(BF16)
    wukv = mla_w_ukv.reshape(C_KV_RANK, C_HEADS, C_NOPE_DIM + C_V_DIM)
    wk = jnp.pad(wukv[:, :, :C_NOPE_DIM], ((0, 0), (0, 0), (0, LANES - C_NOPE_DIM))).reshape(C_KV_RANK, C_W)
    wv = jnp.pad(wukv[:, :, C_NOPE_DIM:], ((0, 0), (0, 0), (0, LANES - C_V_DIM))).reshape(C_KV_RANK, C_W)
    wukv_p = jnp.concatenate([wk, wv], axis=1).astype(BF16)
    gq = jnp.pad(mla_q_norm_g, (0, CQ_PAD - C_Q_RANK)).reshape(1, CQ_PAD)
    gkv = mla_kv_norm_g.reshape(1, C_KV_RANK)

    wo_a = w_out[:a].astype(BF16)
    wo_b = jnp.pad(w_out[a:a + bv], ((0, B_W - bv), (0, 0))).astype(BF16)
    wo_c = w_out[a + bv:].reshape(C_HEADS, C_V_DIM, D_MODEL)
    wo_c = jnp.pad(wo_c, ((0, 0), (0, LANES - C_V_DIM), (0, 0))).reshape(C_W, D_MODEL).astype(BF16)
    return w_proj, gq, wuq, gkv, wukv_p, wo_a, wo_b, wo_c


def _proj_kernel(x_ref, g_ref, w_ref, cs_ref, exp_ref, gq_ref, wuq_ref, gkv_ref, wukv_ref,
                 qa_ref, qb_ref, qc_ref, kc_ref, vc_ref):
    x = x_ref[0]
    ms = jnp.mean(x * x, axis=-1, keepdims=True)
    h = (x * lax.rsqrt(ms + NORM_EPS) * g_ref[...]).astype(BF16)
    p = jnp.dot(h, w_ref[...], preferred_element_type=F32)

    cs = cs_ref[0]
    cs_hi = cs.astype(BF16)
    cs_lo = (cs - cs_hi.astype(F32)).astype(BF16)
    lane = lax.broadcasted_iota(jnp.int32, cs.shape, 1)
    tabs = jnp.dot(jnp.where(lane < _ROPE_COLS, cs_hi, cs_lo), exp_ref[...], preferred_element_type=F32)

    def slab(i):
        return p[:, i * LANES:(i + 1) * LANES]

    def rope(xs, layout, shift):
        c = tabs[:, (3 * layout) * LANES:(3 * layout + 1) * LANES]
        s1 = tabs[:, (3 * layout + 1) * LANES:(3 * layout + 2) * LANES]
        s2 = tabs[:, (3 * layout + 2) * LANES:(3 * layout + 3) * LANES]
        return xs * c + pltpu.roll(xs, LANES - shift, 1) * s1 + pltpu.roll(xs, shift, 1) * s2

    n_a = A_W // LANES
    for i in range(3 * n_a):
        v = slab(i)
        if i < 2 * n_a:
            v = rope(v, 0, HEAD_DIM // ROPE_FRACTION // 2)
        qa_ref[0, :, i * LANES:(i + 1) * LANES] = v
    base = 3 * n_a
    for i in range(3 * B_SLABS):
        v = slab(base + i)
        if i < 2 * B_SLABS:
            v = rope(v, 1, B_QK_DIM // ROPE_FRACTION // 2)
        qb_ref[0, :, i * LANES:(i + 1) * LANES] = v.astype(BF16)
    base += 3 * B_SLABS

    cq = p[:, base * LANES:base * LANES + CQ_PAD]
    msq = jnp.sum(cq * cq, axis=-1, keepdims=True) * (1.0 / C_Q_RANK)
    cqn = (cq * lax.rsqrt(msq + NORM_EPS) * gq_ref[...]).astype(BF16)
    qf = jnp.dot(cqn, wuq_ref[...], preferred_element_type=F32)
    base += CQ_PAD // LANES
    ckv = slab(base)
    mskv = jnp.mean(ckv * ckv, axis=-1, keepdims=True)
    ckvn = (ckv * lax.rsqrt(mskv + NORM_EPS) * gkv_ref[...]).astype(BF16)
    kvf = jnp.dot(ckvn, wukv_ref[...], preferred_element_type=F32)
    kr = rope(slab(base + 1), 2, C_ROPE_DIM // 2)
    for hd in range(C_HEADS):
        sl = slice(hd * LANES, (hd + 1) * LANES)
        qc_ref[0, :, sl] = rope(qf[:, sl], 2, C_ROPE_DIM // 2).astype(BF16)
        kc_ref[0, :, sl] = (kvf[:, sl] + kr).astype(BF16)
        vc_ref[0, :, sl] = kvf[:, C_W + hd * LANES:C_W + (hd + 1) * LANES].astype(BF16)


def _proj_call(x, g, w_proj, cs, expand, gq, wuq, gkv, wukv, tm=512):
    bsz, seq, d = x.shape
    grid = (bsz, seq // tm)
    row = lambda b, i: (b, i, 0)
    const = lambda b, i: (0, 0)
    out_w = (3 * A_W, 3 * B_W, C_W, C_W, C_W)
    out_dt = (F32, BF16, BF16, BF16, BF16)
    return pl.pallas_call(
        _proj_kernel,
        grid=grid,
        in_specs=[pl.BlockSpec((1, tm, d), row),
                  pl.BlockSpec((1, d), const),
                  pl.BlockSpec((d, PROJ_W), const),
                  pl.BlockSpec((1, tm, LANES), row),
                  pl.BlockSpec((LANES, N_TABS * LANES), const),
                  pl.BlockSpec((1, CQ_PAD), const),
                  pl.BlockSpec((CQ_PAD, C_W), const),
                  pl.BlockSpec((1, C_KV_RANK), const),
                  pl.BlockSpec((C_KV_RANK, 2 * C_W), const)],
        out_specs=[pl.BlockSpec((1, tm, w), row) for w in out_w],
        out_shape=[jax.ShapeDtypeStruct((bsz, seq, w), dt) for w, dt in zip(out_w, out_dt)],
        compiler_params=_cparams(("parallel", "parallel")),
        name="proj",
    )(x, g.reshape(1, d), w_proj, cs, expand, gq, wuq, gkv, wukv)


def _head_rmsnorm_t(ot, g_t, eps):
    ss = jnp.sum(ot * ot, axis=0, keepdims=True) * (1.0 / HEAD_DIM)
    return ot * lax.rsqrt(ss + eps) * g_t


def _head_rmsnorm(o, g, eps):
    ri = lax.broadcasted_iota(jnp.int32, (LANES, LANES), 0) // HEAD_DIM
    ci = lax.broadcasted_iota(jnp.int32, (LANES, LANES), 1) // HEAD_DIM
    bd = jnp.where(ri == ci, 1.0, 0.0).astype(BF16)
    sq = o * o
    sq_hi = sq.astype(BF16)
    sq_lo = (sq - sq_hi.astype(F32)).astype(BF16)
    ss = (jnp.dot(sq_hi, bd, preferred_element_type=F32) + jnp.dot(sq_lo, bd, preferred_element_type=F32))
    return o * lax.rsqrt(ss * (1.0 / HEAD_DIM) + eps) * g


_ATT_TQ = 256
_ATT_TK = 512


def _transpose_v(v_ref, vt_scr):
    for c in range(v_ref.shape[1] // _ATT_TK):
        sl = slice(c * _ATT_TK, (c + 1) * _ATT_TK)
        vt_scr[:, sl] = jnp.transpose(v_ref[0, sl, :].astype(F32)).astype(BF16)


def _scores_phase(qm, k_ref, s_scr, m_scr, slot):
    tq = qm.shape[0]
    mx = jnp.full((8, tq), -jnp.inf, F32)
    for c in range(k_ref.shape[1] // _ATT_TK):
        s = lax.dot_general(k_ref[0, c * _ATT_TK:(c + 1) * _ATT_TK, :], qm, _NT, preferred_element_type=F32)
        s_scr[slot, c] = s
        mx = jnp.maximum(mx, jnp.max(s.reshape(_ATT_TK // 8, 8, tq), axis=0))
    m_scr[slot] = mx


def _softmax_phase(vt_scr, rows, s_scr, m_scr, slot):
    m = jnp.max(m_scr[slot], axis=0, keepdims=True)
    tq = m.shape[1]
    lp = jnp.zeros((8, tq), F32)
    acc = jnp.zeros((rows.stop - rows.start, tq), F32)
    for c in range(s_scr.shape[1]):
        p = jnp.exp2(s_scr[slot, c] - m)
        lp = lp + jnp.sum(p.reshape(_ATT_TK // 8, 8, tq), axis=0)
        acc = acc + jnp.dot(vt_scr[rows, c * _ATT_TK:(c + 1) * _ATT_TK], p.astype(BF16),
                            preferred_element_type=F32)
    return acc / jnp.sum(lp, axis=0, keepdims=True)


def _attn_jobs(jobs, q_next, first, k_ref, vt_scr, rows, s_scr, m_scr):
    assert len(jobs) % 2 == 0

    @pl.when(first)
    def _():
        _scores_phase(jobs[0], k_ref, s_scr, m_scr, 0)

    outs = []
    for n in range(len(jobs)):
        nxt = jobs[n + 1] if n + 1 < len(jobs) else q_next
        _scores_phase(nxt, k_ref, s_scr, m_scr, (n + 1) % 2)
        outs.append(_softmax_phase(vt_scr, rows[n], s_scr, m_scr, n % 2))
    return outs


_DIL_TQ = 128
_DIL_TW = 256


def _dil_kernel(q_ref, k_ref, v_ref, g_ref, o_ref, *scr):
    seq = q_ref.shape[1]
    lane = lax.broadcasted_iota(jnp.int32, (1, LANES), 1)
    head0 = lane < HEAD_DIM
    rel = (lax.broadcasted_iota(jnp.int32, (_DIL_TQ, _DIL_TW), 1)
           - lax.broadcasted_iota(jnp.int32, (_DIL_TQ, _DIL_TW), 0))

    for pi, r in enumerate(A_DILATIONS):
        length = seq // r
        nblk = length // _DIL_TQ
        o_scr, m_scr, l_scr = scr[3 * pi:3 * pi + 3]

        def body(t, carry, r=r, length=length, nblk=nblk, o_scr=o_scr, m_scr=m_scr, l_scr=l_scr):
            c = t // nblk
            i = t % nblk
            ws = jnp.clip(i * _DIL_TQ - A_HALF_WINDOW, 0, length - _DIL_TW)
            qs = c + r * (i * _DIL_TQ)
            ks = c + r * ws
            if r == 1:
                qsl = pl.ds(pl.multiple_of(qs, _DIL_TQ), _DIL_TQ)
                ksl = pl.ds(pl.multiple_of(ks, A_HALF_WINDOW), _DIL_TW)
            else:
                qsl = pl.ds(qs, _DIL_TQ, stride=r)
                ksl = pl.ds(ks, _DIL_TW, stride=r)
            qb = q_ref[0, qsl, :]
            kb = k_ref[0, ksl, :].astype(BF16)
            vb = v_ref[0, ksl, :].astype(BF16)
            delta = rel + (ws - i * _DIL_TQ)
            valid = jnp.abs(delta) <= A_HALF_WINDOW
            res = []
            for hd in range(2):
                qm = jnp.where(head0 if hd == 0 else jnp.logical_not(head0), qb, 0.0).astype(BF16)
                s = lax.dot_general(qm, kb, _NT, preferred_element_type=F32)
                s = jnp.where(valid, s, NEG_INF)
                m = jnp.max(s, axis=1, keepdims=True)
                p = jnp.exp(s - m)
                l = jnp.sum(p, axis=1, keepdims=True)
                o = jnp.dot(p.astype(BF16), vb, preferred_element_type=F32)
                res.append((o, m, l))
            o_scr[qsl, :] = jnp.where(head0, res[0][0], res[1][0])
            m_scr[qsl, :] = jnp.where(head0, res[0][1], res[1][1])
            l_scr[qsl, :] = jnp.where(head0, res[0][2], res[1][2])
            return carry

        lax.fori_loop(0, r * nblk, body, 0, unroll=8)

    rows = 256
    g = g_ref[...]

    def combine(t, carry):
        sl = pl.ds(pl.multiple_of(t * rows, rows), rows)
        ms = [scr[3 * pi + 1][sl, :] for pi in range(3)]
        mm = jnp.maximum(jnp.maximum(ms[0], ms[1]), ms[2])
        num = jnp.zeros((rows, LANES), F32)
        den = jnp.zeros((rows, LANES), F32)
        for pi in range(3):
            e = jnp.exp(ms[pi] - mm)
            num = num + e * scr[3 * pi][sl, :]
            den = den + e * scr[3 * pi + 2][sl, :]
        o_ref[0, sl, :] = _head_rmsnorm(num / den, g, NORM_EPS).astype(o_ref.dtype)
        return carry

    lax.fori_loop(0, seq // rows, combine, 0)


def _dil_call(qkv_a, g):
    bsz, seq, _ = qkv_a.shape
    n_slab = A_W // LANES
    blk = (1, seq, LANES)
    return pl.pallas_call(
        _dil_kernel,
        grid=(bsz, n_slab),
        in_specs=[pl.BlockSpec(blk, lambda b, j: (b, 0, j)),
                  pl.BlockSpec(blk, lambda b, j: (b, 0, n_slab + j)),
                  pl.BlockSpec(blk, lambda b, j: (b, 0, 2 * n_slab + j)),
                  pl.BlockSpec((1, LANES), lambda b, j: (0, j))],
        out_specs=pl.BlockSpec(blk, lambda b, j: (b, 0, j)),
        out_shape=jax.ShapeDtypeStruct((bsz, seq, A_W), BF16),
        scratch_shapes=[pltpu.VMEM((seq, LANES), F32) for _ in range(9)],
        compiler_params=_cparams(("parallel", "parallel")),
        name="dilated_attn",
    )(qkv_a, qkv_a, qkv_a, g.reshape(1, A_W))


def _attn_scratch(seq):
    return [pltpu.VMEM((LANES, seq), BF16),
            pltpu.VMEM((2, seq // _ATT_TK, _ATT_TK, _ATT_TQ), F32),
            pltpu.VMEM((2, 8, _ATT_TQ), F32)]


_DIFF_TILES = 2


def _diff_kernel(lam_ref, q_ref, qn_ref, k_ref, v_ref, g_ref, o_ref, vt_scr, s_scr, m_scr):
    first = pl.program_id(2) == 0
    lane = lax.broadcasted_iota(jnp.int32, (1, LANES), 1)
    lam = lam_ref[0]

    def qmap(q, mp):
        lo = mp * B_QK_DIM
        return jnp.where((lane >= lo) & (lane < lo + B_QK_DIM), q, jnp.zeros_like(q))

    @pl.when(first)
    def _():
        _transpose_v(v_ref, vt_scr)

    def run(n_heads):
        nm = 2 * n_heads
        jobs = [qmap(q_ref[0, t * _ATT_TQ:(t + 1) * _ATT_TQ, :], mp)
                for t in range(_DIFF_TILES) for mp in range(nm)]
        rows = [slice((mp // 2) * HEAD_DIM, (mp // 2 + 1) * HEAD_DIM) for mp in range(nm)] * _DIFF_TILES
        outs = _attn_jobs(jobs, qmap(qn_ref[0], 0), first, k_ref, vt_scr, rows, s_scr, m_scr)
        for t in range(_DIFF_TILES):
            heads = [_head_rmsnorm_t(outs[t * nm + 2 * hd] - lam * outs[t * nm + 2 * hd + 1],
                                     g_ref[hd * HEAD_DIM:(hd + 1) * HEAD_DIM, :], SUBLN_EPS)
                     for hd in range(n_heads)]
            if n_heads == 1:
                heads.append(jnp.zeros_like(heads[0]))
            o_ref[0, t * _ATT_TQ:(t + 1) * _ATT_TQ, :] = jnp.transpose(
                jnp.concatenate(heads, axis=0)).astype(o_ref.dtype)

    @pl.when(pl.program_id(1) < B_HEADS // 2)
    def _():
        run(2)

    @pl.when(pl.program_id(1) == B_HEADS // 2)
    def _():
        run(1)


def _diff_call(qkv_b, lam, g):
    bsz, seq, _ = qkv_b.shape
    nq = seq // (_DIFF_TILES * _ATT_TQ)
    kv = (1, seq, LANES)
    last = seq // _ATT_TQ - 1
    return pl.pallas_call(
        _diff_kernel,
        grid=(bsz, B_SLABS, nq),
        in_specs=[pl.BlockSpec(memory_space=pltpu.SMEM),
                  pl.BlockSpec((1, _DIFF_TILES * _ATT_TQ, LANES), lambda b, j, i: (b, i, j)),
                  pl.BlockSpec((1, _ATT_TQ, LANES),
                               lambda b, j, i: (b, jnp.minimum(_DIFF_TILES * (i + 1), last), j)),
                  pl.BlockSpec(kv, lambda b, j, i: (b, 0, B_SLABS + j)),
                  pl.BlockSpec(kv, lambda b, j, i: (b, 0, 2 * B_SLABS + j)),
                  pl.BlockSpec((LANES, _ATT_TQ), lambda b, j, i: (0, 0))],
        out_specs=pl.BlockSpec((1, _DIFF_TILES * _ATT_TQ, LANES), lambda b, j, i: (b, i, j)),
        out_shape=jax.ShapeDtypeStruct((bsz, seq, B_W), BF16),
        scratch_shapes=_attn_scratch(seq),
        compiler_params=_cparams(("parallel", "parallel", "arbitrary")),
        name="diff_attn",
    )(lam, qkv_b, qkv_b, qkv_b, qkv_b, g)


_MLA_TILES = 8


def _mla_kernel(q_ref, qn_ref, k_ref, v_ref, g_ref, o_ref, vt_scr, s_scr, m_scr):
    first = pl.program_id(2) == 0

    @pl.when(first)
    def _():
        _transpose_v(v_ref, vt_scr)

    jobs = [q_ref[0, n * _ATT_TQ:(n + 1) * _ATT_TQ, :] for n in range(_MLA_TILES)]
    rows = [slice(0, C_V_DIM)] * _MLA_TILES
    outs = _attn_jobs(jobs, qn_ref[0], first, k_ref, vt_scr, rows, s_scr, m_scr)
    for n, o in enumerate(outs):
        on = _head_rmsnorm_t(o, g_ref[:C_V_DIM, :], NORM_EPS)
        ot = jnp.concatenate([on, jnp.zeros_like(on)], axis=0)
        o_ref[0, n * _ATT_TQ:(n + 1) * _ATT_TQ, :] = jnp.transpose(ot).astype(o_ref.dtype)


def _mla_call(qc, kc, vc, g):
    bsz, seq, _ = qc.shape
    nq = seq // (_MLA_TILES * _ATT_TQ)
    kv = (1, seq, LANES)
    last = seq // _ATT_TQ - 1
    return pl.pallas_call(
        _mla_kernel,
        grid=(bsz, C_HEADS, nq),
        in_specs=[pl.BlockSpec((1, _MLA_TILES * _ATT_TQ, LANES), lambda b, j, i: (b, i, j)),
                  pl.BlockSpec((1, _ATT_TQ, LANES),
                               lambda b, j, i: (b, jnp.minimum(_MLA_TILES * (i + 1), last), j)),
                  pl.BlockSpec(kv, lambda b, j, i: (b, 0, j)),
                  pl.BlockSpec(kv, lambda b, j, i: (b, 0, j)),
                  pl.BlockSpec((LANES, _ATT_TQ), lambda b, j, i: (j, 0))],
        out_specs=pl.BlockSpec((1, _MLA_TILES * _ATT_TQ, LANES), lambda b, j, i: (b, i, j)),
        out_shape=jax.ShapeDtypeStruct((bsz, seq, C_W), BF16),
        scratch_shapes=_attn_scratch(seq),
        compiler_params=_cparams(("parallel", "parallel", "arbitrary")),
        name="latent_attn",
    )(qc, qc, kc, vc, g)


def _out_kernel(oa_ref, ob_ref, oc_ref, x_ref, wa_ref, wb_ref, wc_ref, g_ref, wr_ref,
                x1_ref, h_ref, lg_ref):
    y = jnp.dot(oa_ref[0], wa_ref[...], preferred_element_type=F32)
    y = y + jnp.dot(ob_ref[0], wb_ref[...], preferred_element_type=F32)
    y = y + jnp.dot(oc_ref[0], wc_ref[...], preferred_element_type=F32)
    x1 = x_ref[0] + y
    x1_ref[0] = x1
    ms = jnp.mean(x1 * x1, axis=-1, keepdims=True)
    h = x1 * lax.rsqrt(ms + NORM_EPS) * g_ref[...]
    h_hi = h.astype(BF16)
    h_ref[0] = h_hi
    h_lo = (h - h_hi.astype(F32)).astype(BF16)
    t = jnp.dot(h_hi, wr_ref[...], preferred_element_type=F32)
    lg_ref[0] = (t[:, :LANES] + t[:, LANES:]
                 + jnp.dot(h_lo, wr_ref[:, :LANES], preferred_element_type=F32))


def _out_call(oa, ob, oc, x, wo_a, wo_b, wo_c, g, w_router, tm=512):
    bsz, seq, d = x.shape
    row = lambda b, i: (b, i, 0)
    const = lambda b, i: (0, 0)
    wr = _pad_cols(w_router, LANES)
    wr_hi = wr.astype(BF16)
    wr = jnp.concatenate([wr_hi, (wr - wr_hi.astype(F32)).astype(BF16)], axis=1)
    return pl.pallas_call(
        _out_kernel,
        grid=(bsz, seq // tm),
        in_specs=[pl.BlockSpec((1, tm, A_W), row),
                  pl.BlockSpec((1, tm, B_W), row),
                  pl.BlockSpec((1, tm, C_W), row),
                  pl.BlockSpec((1, tm, d), row),
                  pl.BlockSpec((A_W, d), const),
                  pl.BlockSpec((B_W, d), const),
                  pl.BlockSpec((C_W, d), const),
                  pl.BlockSpec((1, d), const),
                  pl.BlockSpec((d, 2 * LANES), const)],
        out_specs=[pl.BlockSpec((1, tm, d), row),
                   pl.BlockSpec((1, tm, d), row),
                   pl.BlockSpec((1, tm, LANES), row)],
        out_shape=[jax.ShapeDtypeStruct((bsz, seq, d), F32),
                   jax.ShapeDtypeStruct((bsz, seq, d), BF16),
                   jax.ShapeDtypeStruct((bsz, seq, LANES), F32)],
        compiler_params=_cparams(("parallel", "parallel")),
        name="out_proj",
    )(oa, ob, oc, x, wo_a, wo_b, wo_c, g.reshape(1, d), wr)


_MOE_CHUNK = 256


def _router_kernel(lg_ref, pos_ref, post_ref, aff_ref, bnd_ref, pre_scr, *, cap):
    seq = lg_ref.shape[1]
    lt = jnp.transpose(lg_ref[0])[:N_EXPERTS]
    ex = jnp.exp(lt - jnp.max(lt, axis=0, keepdims=True))
    aff = ex / jnp.sum(ex, axis=0, keepdims=True)
    bits = pltpu.bitcast(aff, jnp.int32)

    def step(k, v):
        cand = v | lax.shift_left(jnp.int32(1), 30 - k)
        cnt = jnp.sum((bits >= cand).astype(jnp.int32), axis=1, keepdims=True)
        return jnp.where(cnt >= cap, cand, v)

    kth = lax.fori_loop(0, 31, step, jnp.zeros((N_EXPERTS, 1), jnp.int32))
    gt = bits > kth
    eq = bits == kth
    need = (cap - jnp.sum(gt.astype(jnp.int32), axis=1, keepdims=True)).astype(F32)

    flags = jnp.concatenate([gt, eq], axis=0)
    tri = (lax.broadcasted_iota(jnp.int32, (LANES, LANES), 0)
           <= lax.broadcasted_iota(jnp.int32, (LANES, LANES), 1)).astype(BF16)
    carry = jnp.zeros((2 * N_EXPERTS, 1), F32)
    bnd_ref[0] = jnp.zeros((N_EXPERTS, LANES), F32)
    for j in range(seq // LANES):
        if j % (_MOE_CHUNK // LANES) == 0:
            t = j // (_MOE_CHUNK // LANES)
            bnd_ref[0, :, t:t + 1] = carry[:N_EXPERTS] + jnp.minimum(carry[N_EXPERTS:], need)
        blk = jnp.where(flags[:, j * LANES:(j + 1) * LANES], 1.0, 0.0)
        inc = jnp.dot(blk.astype(BF16), tri, preferred_element_type=F32)
        pre_scr[:, j * LANES:(j + 1) * LANES] = inc - blk + carry
        carry = carry + inc[:, LANES - 1:LANES]
    n_chunks = seq // _MOE_CHUNK
    bnd_ref[0, :, n_chunks:n_chunks + 1] = carry[:N_EXPERTS] + jnp.minimum(carry[N_EXPERTS:], need)
    pre_gt = pre_scr[:N_EXPERTS]
    pre_eq = pre_scr[N_EXPERTS:]
    sel = jnp.logical_or(gt, jnp.logical_and(eq, pre_eq < need))
    pos = jnp.where(sel, pre_gt + jnp.minimum(pre_eq, need), -1.0)
    for c in range(n_chunks):
        sl = slice(c * _MOE_CHUNK, (c + 1) * _MOE_CHUNK)
        pos_ref[0, c] = pos[:, sl].astype(jnp.int32)
        aff_ref[0, c] = aff[:, sl]
    pad = jnp.full((LANES - N_EXPERTS, seq), -1.0, F32)
    post_ref[0] = jnp.transpose(jnp.concatenate([pos, pad], axis=0))


def _router_call(logits, cap):
    bsz, seq, _ = logits.shape
    chunked = (1, seq // _MOE_CHUNK, N_EXPERTS, _MOE_CHUNK)
    return pl.pallas_call(
        functools.partial(_router_kernel, cap=cap),
        grid=(bsz,),
        in_specs=[pl.BlockSpec((1, seq, LANES), lambda b: (b, 0, 0))],
        out_specs=[pl.BlockSpec(chunked, lambda b: (b, 0, 0, 0)),
                   pl.BlockSpec((1, seq, LANES), lambda b: (b, 0, 0)),
                   pl.BlockSpec(chunked, lambda b: (b, 0, 0, 0)),
                   pl.BlockSpec((1, N_EXPERTS, LANES), lambda b: (b, 0, 0))],
        out_shape=[jax.ShapeDtypeStruct((bsz,) + chunked[1:], jnp.int32),
                   jax.ShapeDtypeStruct((bsz, seq, LANES), F32),
                   jax.ShapeDtypeStruct((bsz,) + chunked[1:], F32),
                   jax.ShapeDtypeStruct((bsz, N_EXPERTS, LANES), F32)],
        scratch_shapes=[pltpu.VMEM((2 * N_EXPERTS, seq), F32)],
        compiler_params=_cparams(("parallel",)),
        name="router",
    )(logits)


def _gather_kernel(pos_ref, aff_ref, h_ref, xe_ref, gate_ref, p_scr, *, cap):
    e = pl.program_id(1)
    slot = lax.broadcasted_iota(jnp.int32, (cap, _MOE_CHUNK), 0)
    gate = jnp.zeros((cap, 1), F32)
    for c in range(pos_ref.shape[1]):
        hit = pos_ref[0, c, pl.ds(e, 1), :] == slot
        p_scr[:, c * _MOE_CHUNK:(c + 1) * _MOE_CHUNK] = jnp.where(hit, 1.0, 0.0).astype(BF16)
        gate = gate + jnp.sum(jnp.where(hit, aff_ref[0, c, pl.ds(e, 1), :], 0.0), axis=1, keepdims=True)
    xe_ref[0] = jnp.dot(p_scr[...], h_ref[0], preferred_element_type=F32).astype(BF16)
    gate_ref[0] = gate


def _gather_call(pos, aff, h_bf16, cap):
    bsz, seq, d = h_bf16.shape
    chunked = (1, seq // _MOE_CHUNK, N_EXPERTS, _MOE_CHUNK)
    return pl.pallas_call(
        functools.partial(_gather_kernel, cap=cap),
        grid=(bsz, N_EXPERTS),
        in_specs=[pl.BlockSpec(chunked, lambda b, e: (b, 0, 0, 0)),
                  pl.BlockSpec(chunked, lambda b, e: (b, 0, 0, 0)),
                  pl.BlockSpec((1, seq, d), lambda b, e: (b, 0, 0))],
        out_specs=[pl.BlockSpec((1, cap, d), lambda b, e: (e, b, 0)),
                   pl.BlockSpec((1, cap, 1), lambda b, e: (e, b, 0))],
        out_shape=[jax.ShapeDtypeStruct((N_EXPERTS, bsz * cap, d), BF16),
                   jax.ShapeDtypeStruct((N_EXPERTS, bsz * cap, 1), F32)],
        scratch_shapes=[pltpu.VMEM((cap, seq), BF16)],
        compiler_params=_cparams(("parallel", "arbitrary")),
        name="moe_gather",
    )(pos, aff, h_bf16)


def _expert_kernel(xe_ref, gate_ref, wg_ref, wu_ref, wd_ref, y_ref, wg_scr, wu_scr, wd_scr):
    @pl.when(pl.program_id(1) == 0)
    def _():
        wg_scr[...] = wg_ref[0].astype(BF16)
        wu_scr[...] = wu_ref[0].astype(BF16)
        wd_scr[...] = wd_ref[0].astype(BF16)

    xe = xe_ref[0]
    a = jnp.dot(xe, wg_scr[...], preferred_element_type=F32)
    u = jnp.dot(xe, wu_scr[...], preferred_element_type=F32)
    hmid = (a * jax.nn.sigmoid(a) * u).astype(BF16)
    y = jnp.dot(hmid, wd_scr[...], preferred_element_type=F32)
    y_ref[0] = (y * gate_ref[0]).astype(y_ref.dtype)


def _expert_call(xe, gate, w_gate, w_up, w_down, layer, cap):
    n_e, m, d = xe.shape
    bsz = m // cap
    first = layer * n_e
    return pl.pallas_call(
        _expert_kernel,
        grid=(n_e, bsz),
        in_specs=[pl.BlockSpec((1, cap, d), lambda e, b: (e, b, 0)),
                  pl.BlockSpec((1, cap, 1), lambda e, b: (e, b, 0)),
                  pl.BlockSpec((1, d, EXPERT_FF), lambda e, b: (first + e, 0, 0)),
                  pl.BlockSpec((1, d, EXPERT_FF), lambda e, b: (first + e, 0, 0)),
                  pl.BlockSpec((1, EXPERT_FF, d), lambda e, b: (first + e, 0, 0))],
        out_specs=pl.BlockSpec((1, cap, d), lambda e, b: (b, e, 0)),
        out_shape=jax.ShapeDtypeStruct((bsz, n_e * cap, d), BF16),
        scratch_shapes=[pltpu.VMEM((d, EXPERT_FF), BF16), pltpu.VMEM((d, EXPERT_FF), BF16),
                        pltpu.VMEM((EXPERT_FF, d), BF16)],
        compiler_params=_cparams(("parallel", "arbitrary")),
        name="experts",
    )(xe, gate, w_gate, w_up, w_down)


_SCATTER_WIN = 256
_WIN_ALIGN = 16


def _scatter_kernel(win_ref, post_ref, x1_ref, y_ref, g_ref, o_ref, *, cap, final_norm):
    base = ((pl.program_id(0) * pl.num_programs(1) + pl.program_id(1)) * N_EXPERTS) * 2
    lane = lax.broadcasted_iota(jnp.int32, (_MOE_CHUNK, _SCATTER_WIN), 1)
    slot = lane.astype(F32)
    half = _SCATTER_WIN // 2

    def onehot(col):
        return jnp.where(col == slot, 1.0, 0.0).astype(BF16)

    def window(e):
        w0 = win_ref[base + 2 * e]
        rel = post_ref[0, :, e:e + 1] - w0.astype(F32)
        return rel, y_ref[0, pl.ds(pl.multiple_of(e * cap + w0, _WIN_ALIGN), half), :]

    acc = x1_ref[0]
    for e in range(0, N_EXPERTS, 2):
        rel_a, y_a = window(e)
        rel_b, y_b = window(e + 1)
        p = onehot(jnp.where(lane < half, rel_a, rel_b + half))
        acc = acc + jnp.dot(p, jnp.concatenate([y_a, y_b], axis=0), preferred_element_type=F32)
    o_ref[0] = acc

    for e in range(N_EXPERTS):
        @pl.when(win_ref[base + 2 * e + 1] != 0)
        def _(e=e):
            done = win_ref[base + 2 * e] + half
            w1 = jnp.minimum(done, cap - _SCATTER_WIN)
            col = post_ref[0, :, e:e + 1]
            col = jnp.where(col >= done.astype(F32), col - w1.astype(F32), -1.0)
            rows = pl.ds(pl.multiple_of(e * cap + w1, _WIN_ALIGN), _SCATTER_WIN)
            o_ref[0] += jnp.dot(onehot(col), y_ref[0, rows, :], preferred_element_type=F32)

    if final_norm:
        x = o_ref[0]
        ms = jnp.mean(x * x, axis=-1, keepdims=True)
        o_ref[0] = x * lax.rsqrt(ms + NORM_EPS) * g_ref[...]


def _scatter_call(win, post, x1, y, g, cap, final_norm):
    bsz, seq, d = x1.shape
    assert cap <= 2 * _SCATTER_WIN
    return pl.pallas_call(
        functools.partial(_scatter_kernel, cap=cap, final_norm=final_norm),
        grid_spec=pltpu.PrefetchScalarGridSpec(
            num_scalar_prefetch=1,
            grid=(bsz, seq // _MOE_CHUNK),
            in_specs=[pl.BlockSpec((1, _MOE_CHUNK, LANES), lambda b, t, win: (b, t, 0)),
                      pl.BlockSpec((1, _MOE_CHUNK, d), lambda b, t, win: (b, t, 0)),
                      pl.BlockSpec((1, N_EXPERTS * cap, d), lambda b, t, win: (b, 0, 0)),
                      pl.BlockSpec((1, d), lambda b, t, win: (0, 0))],
            out_specs=pl.BlockSpec((1, _MOE_CHUNK, d), lambda b, t, win: (b, t, 0))),
        out_shape=jax.ShapeDtypeStruct((bsz, seq, d), F32),
        compiler_params=_cparams(("parallel", "arbitrary")),
        name="moe_scatter",
    )(win, post, x1, y, g.reshape(1, d))


def _expert_choice(h, logits, x1, w_gate, w_up, w_down, layer, final_g, final_norm):
    seq = h.shape[1]
    cap = CAPACITY_FACTOR * seq // N_EXPERTS
    pos, post, aff, bnd = _router_call(logits, cap)
    bnd = bnd.astype(jnp.int32)
    n_chunks = seq // _MOE_CHUNK
    start, stop = bnd[:, :, :n_chunks], bnd[:, :, 1:n_chunks + 1]
    half = _SCATTER_WIN // 2
    w0 = jnp.minimum(start // _WIN_ALIGN * _WIN_ALIGN, cap - half)
    win = jnp.stack([w0, (stop > w0 + half).astype(jnp.int32)], axis=-1)
    win = jnp.swapaxes(win, 1, 2)
    xe, gate = _gather_call(pos, aff, h, cap)
    y = _expert_call(xe, gate, w_gate, w_up, w_down, layer, cap)
    return _scatter_call(win.reshape(-1), post, x1, y, final_g, cap, final_norm)


def kernel(x, positions, attn_norm_g, w_in, lam_q1, lam_k1, lam_q2, lam_k2, diff_subln_g, mla_q_norm_g, mla_w_uq, mla_kv_norm_g, mla_w_ukv, dil_out_g, mla_out_g, w_out, ffn_norm_g, w_router, w_gate, w_up, w_down, final_norm_g):
    depth = w_in.shape[0]
    wg_all = w_gate.reshape((depth * N_EXPERTS,) + w_gate.shape[2:])
    wu_all = w_up.reshape((depth * N_EXPERTS,) + w_up.shape[2:])
    wd_all = w_down.reshape((depth * N_EXPERTS,) + w_down.shape[2:])
    cs = _rope_compact(positions)
    expand = _rope_expansion()
    for l in range(depth):
        lam_init = 0.8 - 0.6 * math.exp(-0.3 * l)
        w_proj, gq, wuq, gkv, wukv, wo_a, wo_b, wo_c = _prep_layer(
            w_in[l], mla_q_norm_g[l], mla_w_uq[l], mla_kv_norm_g[l], mla_w_ukv[l], w_out[l])
        qkv_a, qkv_b, qc, kc, vc = _proj_call(x, attn_norm_g[l], w_proj, cs, expand, gq, wuq, gkv, wukv)

        oa = _dil_call(qkv_a, dil_out_g[l])
        lam = (jnp.exp(jnp.sum(lam_q1[l] * lam_k1[l])) - jnp.exp(jnp.sum(lam_q2[l] * lam_k2[l])) + lam_init)
        g_b = jnp.tile(diff_subln_g[l] * (1.0 - lam_init), LANES // B_V_DIM)
        g_b = jnp.broadcast_to(g_b[:, None], (LANES, _ATT_TQ))
        ob = _diff_call(qkv_b, lam.reshape(1), g_b)
        g_c = jnp.pad(mla_out_g[l].reshape(C_HEADS, C_V_DIM), ((0, 0), (0, LANES - C_V_DIM))).reshape(C_W)
        g_c = jnp.broadcast_to(g_c[:, None], (C_W, _ATT_TQ))
        oc = _mla_call(qc, kc, vc, g_c)

        x1, h, logits = _out_call(oa, ob, oc, x, wo_a, wo_b, wo_c, ffn_norm_g[l], w_router[l])
        x = _expert_choice(h, logits, x1, wg_all, wu_all, wd_all, l, final_norm_g, l == depth - 1)
    return x
```
